```python
import math
import jax, jax.numpy as jnp
from jax import lax
import numpy as np

D_MODEL = 2048
BATCH = 2
SEQ = 16384
DEPTH = 4

GRID_W = 64
CTX_LEN = 256
N_MIXERS = 2
N_LAYERS_A = (DEPTH + N_MIXERS - 1) // N_MIXERS
N_LAYERS_B = DEPTH // N_MIXERS
D_FF = 4 * D_MODEL
DIFF_HEAD_DIM = 64
DIFF_HEADS = D_MODEL // (2 * DIFF_HEAD_DIM)
MLA_HEADS = D_MODEL // 128
MLA_NOPE = 128
MLA_ROPE = 64
MLA_V = 128
MLA_Q_LORA = 448
MLA_KV_LORA = 512
MLA_DOWN = MLA_Q_LORA + MLA_KV_LORA + MLA_ROPE
DIFF_SCALE = DIFF_HEAD_DIM ** -0.5
MLA_SCALE = (MLA_NOPE + MLA_ROPE) ** -0.5
ROPE_THETA = 10000.0
Q_BLOCK = 128
EPS = 1e-6

kernel_name = "hybrid_diffattn_mla_dit"


def rmsnorm(x, g):
    x32 = x.astype(jnp.float32)
    y = x32 * lax.rsqrt(jnp.mean(jnp.square(x32), axis=-1, keepdims=True) + EPS)
    return (y * g.astype(jnp.float32)).astype(x.dtype)


def modulate(h, shift, scale):
    return h * (1 + scale) + shift


def axial_rope_tables(row, col, r):
    q = r // 4
    inv = ROPE_THETA ** (-jnp.arange(q, dtype=jnp.float32) / q)
    ang = jnp.stack([row.astype(jnp.float32)[:, None] * inv, col.astype(jnp.float32)[:, None] * inv], axis=1)
    n = row.shape[0]
    ang = jnp.broadcast_to(ang[:, :, None, :], (n, 2, 2, q)).reshape(n, r)
    return jnp.cos(ang), jnp.sin(ang)


def apply_rope(x, cos, sin):
    r = x.shape[-1]
    xs = x.reshape(x.shape[:-1] + (2, 2, r // 4))
    rot = jnp.stack([-xs[..., 1, :], xs[..., 0, :]], axis=-2).reshape(x.shape)
    return (x.astype(jnp.float32) * cos + rot.astype(jnp.float32) * sin).astype(x.dtype)


def sweep_query_blocks(fn, *qs):
    b, n = qs[0].shape[:2]
    nb = n // Q_BLOCK
    blocks = tuple(jnp.moveaxis(q.reshape((b, nb, Q_BLOCK) + q.shape[2:]), 1, 0) for q in qs)
    out = lax.map(lambda qb: fn(*qb), blocks)
    return jnp.moveaxis(out, 0, 1).reshape((b, n) + out.shape[3:])


def diff_attend(q, k, v, lam):
    s = jnp.einsum('bqhcd,bkhcd->bhcqk', q, k) * DIFF_SCALE
    p = jax.nn.softmax(s.astype(jnp.float32), axis=-1)
    a = p[:, :, 0] - lam * p[:, :, 1]
    return jnp.einsum('bhqk,bkhe->bqhe', a.astype(v.dtype), v)


def differential_attention(h_lat, h_ctx, wqkv, wo, lam_vec, subln_g, lam_init, cos, sin, with_ctx_out):
    H, Dh = DIFF_HEADS, DIFF_HEAD_DIM

    def project(h):
        b, n, _ = h.shape
        q, k, v = jnp.split(h @ wqkv, 3, axis=-1)
        return q.reshape(b, n, H, 2, Dh), k.reshape(b, n, H, 2, Dh), v.reshape(b, n, H, 2 * Dh)

    lf = lam_vec.astype(jnp.float32)
    lam = jnp.exp(jnp.sum(lf[0] * lf[1])) - jnp.exp(jnp.sum(lf[2] * lf[3])) + lam_init

    q_l, k_l, v_l = project(h_lat)
    q_c, k_c, v_c = project(h_ctx)
    rc, rs = cos[None, :, None, None, :], sin[None, :, None, None, :]
    q_l = apply_rope(q_l, rc, rs)
    k_l = apply_rope(k_l, rc, rs)
    k_all = jnp.concatenate([k_c, k_l], axis=1)
    v_all = jnp.concatenate([v_c, v_l], axis=1)
    o_lat = sweep_query_blocks(lambda qb: diff_attend(qb, k_all, v_all, lam), q_l)

    def finish(o):
        b, n = o.shape[:2]
        o = rmsnorm(o, subln_g) * (1 - lam_init)
        return o.reshape(b, n, H * 2 * Dh) @ wo

    out_ctx = finish(diff_attend(q_c, k_c, v_c, lam)) if with_ctx_out else None
    return finish(o_lat), out_ctx


def mla_attend(qn, qp, kn, kp, v):
    s = (jnp.einsum('bqhd,bkhd->bhqk', qn, kn) + jnp.einsum('bqhr,bkr->bhqk', qp, kp)) * MLA_SCALE
    p = jax.nn.softmax(s.astype(jnp.float32), axis=-1)
    return jnp.einsum('bhqk,bkhd->bqhd', p.astype(v.dtype), v)


def latent_attention(h_lat, h_ctx, wdown, q_norm_g, wuq, kv_norm_g, wukv, wo, cos, sin, with_ctx_out):
    H = MLA_HEADS

    def down(h):
        a = h @ wdown
        cq = rmsnorm(a[..., :MLA_Q_LORA], q_norm_g)
        ckv = rmsnorm(a[..., MLA_Q_LORA:MLA_Q_LORA + MLA_KV_LORA], kv_norm_g)
        k_pe = a[..., MLA_Q_LORA + MLA_KV_LORA:]
        return cq, ckv, k_pe

    def up_q(cq):
        b, n, _ = cq.shape
        q = (cq @ wuq).reshape(b, n, H, MLA_NOPE + MLA_ROPE)
        return q[..., :MLA_NOPE], q[..., MLA_NOPE:]

    def up_kv(ckv):
        b, n, _ = ckv.shape
        kv = (ckv @ wukv).reshape(b, n, H, MLA_NOPE + MLA_V)
        return kv[..., :MLA_NOPE], kv[..., MLA_NOPE:]

    def out_proj(o):
        b, n = o.shape[:2]
        return o.reshape(b, n, H * MLA_V) @ wo

    cq_l, ckv_l, kpe_l = down(h_lat)
    cq_c, ckv_c, kpe_c = down(h_ctx)
    qn_l, qp_l = up_q(cq_l)
    qp_l = apply_rope(qp_l, cos[None, :, None, :], sin[None, :, None, :])
    kpe_l = apply_rope(kpe_l, cos[None], sin[None])
    kn_l, v_l = up_kv(ckv_l)
    kn_c, v_c = up_kv(ckv_c)
    kn_all = jnp.concatenate([kn_c, kn_l], axis=1)
    kpe_all = jnp.concatenate([kpe_c, kpe_l], axis=1)
    v_all = jnp.concatenate([v_c, v_l], axis=1)
    o_lat = sweep_query_blocks(lambda qn, qp: mla_attend(qn, qp, kn_all, kpe_all, v_all), qn_l, qp_l)
    out_ctx = None
    if with_ctx_out:
        qn_c, qp_c = up_q(cq_c)
        out_ctx = out_proj(mla_attend(qn_c, qp_c, kn_c, kpe_c, v_c))
    return out_proj(o_lat), out_ctx


def squared_relu_mlp(h, w1, w2):
    return jnp.square(jax.nn.relu(h @ w1)) @ w2


def diff_lambda_init(layer):
    return 0.8 - 0.6 * math.exp(-0.3 * layer)


def setup_inputs(seed: int = 0) -> dict:
    key = jax.random.key(seed)
    ks = jax.random.split(key, 21)
    f32 = jnp.float32

    def nrm(k, shape, scale):
        return jax.random.normal(k, shape, f32) * scale

    def gain(k, shape):
        return 1.0 + 0.05 * jax.random.normal(k, shape, f32)

    D = D_MODEL
    return {
        "x": nrm(ks[0], (BATCH, SEQ, D), 1.0),
        "c": nrm(ks[1], (BATCH, D), 1.0),
        "ctx": nrm(ks[2], (BATCH, CTX_LEN, D), 1.0),
        "c_ctx": nrm(ks[3], (D,), 1.0),
        "ada_w": nrm(ks[4], (DEPTH, D, 6 * D), 0.5 * D ** -0.5),
        "ada_b": nrm(ks[5], (DEPTH, 6 * D), 0.02),
        "norm_mix_g": gain(ks[6], (DEPTH, D)),
        "norm_mlp_g": gain(ks[7], (DEPTH, D)),
        "dif_wqkv": nrm(ks[8], (N_LAYERS_A, D, 3 * D), D ** -0.5),
        "dif_wo": nrm(ks[9], (N_LAYERS_A, D, D), D ** -0.5),
        "dif_lambda": nrm(ks[10], (N_LAYERS_A, 4, DIFF_HEAD_DIM), 0.1),
        "dif_subln_g": gain(ks[11], (N_LAYERS_A, 2 * DIFF_HEAD_DIM)),
        "mla_wdown": nrm(ks[12], (N_LAYERS_B, D, MLA_DOWN), D ** -0.5),
        "mla_q_norm_g": gain(ks[13], (N_LAYERS_B, MLA_Q_LORA)),
        "mla_wuq": nrm(ks[14], (N_LAYERS_B, MLA_Q_LORA, MLA_HEADS * (MLA_NOPE + MLA_ROPE)), MLA_Q_LORA ** -0.5),
        "mla_kv_norm_g": gain(ks[15], (N_LAYERS_B, MLA_KV_LORA)),
        "mla_wukv": nrm(ks[16], (N_LAYERS_B, MLA_KV_LORA, MLA_HEADS * (MLA_NOPE + MLA_V)), MLA_KV_LORA ** -0.5),
        "mla_wo": nrm(ks[17], (N_LAYERS_B, MLA_HEADS * MLA_V, D), (MLA_HEADS * MLA_V) ** -0.5),
        "mlp_w1": nrm(ks[18], (DEPTH, D, D_FF), D ** -0.5),
        "mlp_w2": nrm(ks[19], (DEPTH, D_FF, D), D_FF ** -0.5),
        "final_g": gain(ks[20], (D,)),
    }


def reference(x, c, ctx, c_ctx, ada_w, ada_b, norm_mix_g, norm_mlp_g, dif_wqkv, dif_wo, dif_lambda, dif_subln_g,
              mla_wdown, mla_q_norm_g, mla_wuq, mla_kv_norm_g, mla_wukv, mla_wo, mlp_w1, mlp_w2, final_g):
    L = x.shape[1]
    ROWS = L // GRID_W
    t_row = jnp.broadcast_to(jnp.arange(ROWS)[:, None], (ROWS, GRID_W)).reshape(-1)
    t_col = jnp.broadcast_to(jnp.arange(GRID_W)[None, :], (ROWS, GRID_W)).reshape(-1)
    cos_a, sin_a = axial_rope_tables(t_row, t_col, DIFF_HEAD_DIM)
    cos_b, sin_b = axial_rope_tables(t_row, t_col, MLA_ROPE)

    s_lat = jax.nn.silu(c)
    s_ctx = jax.nn.silu(c_ctx)[None, :]

    for i in range(DEPTH):
        with_ctx = i < DEPTH - 1
        mod_lat = (s_lat @ ada_w[i] + ada_b[i])[:, None, :]
        mod_ctx = (s_ctx @ ada_w[i] + ada_b[i])[:, None, :]
        sh_a, sc_a, g_a, sh_m, sc_m, g_m = jnp.split(mod_lat, 6, axis=-1)
        csh_a, csc_a, cg_a, csh_m, csc_m, cg_m = jnp.split(mod_ctx, 6, axis=-1)

        h_lat = modulate(rmsnorm(x, norm_mix_g[i]), sh_a, sc_a)
        h_ctx = modulate(rmsnorm(ctx, norm_mix_g[i]), csh_a, csc_a)
        j = i // N_MIXERS
        if i % N_MIXERS == 0:
            o_lat, o_ctx = differential_attention(h_lat, h_ctx, dif_wqkv[j], dif_wo[j], dif_lambda[j],
                                                  dif_subln_g[j], diff_lambda_init(i), cos_a, sin_a, with_ctx)
        else:
            o_lat, o_ctx = latent_attention(h_lat, h_ctx, mla_wdown[j], mla_q_norm_g[j], mla_wuq[j],
                                            mla_kv_norm_g[j], mla_wukv[j], mla_wo[j], cos_b, sin_b, with_ctx)
        x = x + g_a * o_lat
        x = x + g_m * squared_relu_mlp(modulate(rmsnorm(x, norm_mlp_g[i]), sh_m, sc_m), mlp_w1[i], mlp_w2[i])
        if with_ctx:
            ctx = ctx + cg_a * o_ctx
            ctx = ctx + cg_m * squared_relu_mlp(modulate(rmsnorm(ctx, norm_mlp_g[i]), csh_m, csc_m),
                                                mlp_w1[i], mlp_w2[i])

    return rmsnorm(x, final_g)
```

```python
import functools
import math

import jax
import jax.numpy as jnp
from jax import lax
from jax.experimental import pallas as pl
from jax.experimental.pallas import tpu as pltpu

F32 = jnp.float32
BF16 = jnp.bfloat16

GRID_W = 64
N_MIXERS = 2
HEADS = 16
HEAD_W = 128
DIFF_HEAD_DIM = 64
MLA_NOPE = 128
MLA_ROPE = 64
MLA_Q_LORA = 448
MLA_KV_LORA = 512
MLA_QK_W = 256
DIFF_SCALE = DIFF_HEAD_DIM ** -0.5
MLA_SCALE = (MLA_NOPE + MLA_ROPE) ** -0.5
ROPE_THETA = 10000.0
ROPE_QUARTER = 16
EPS = 1e-6

ROW_TILE = 512
MLA_ROW_TILE = 256
ADA_COL_TILE = 1024
QKV_COL_TILE = 512
FF_TILE = 1024
DIFF_Q_TILE = 256
MLA_Q_TILE = 512
KEY_TILE = 512
VMEM_LIMIT = 56 * 1024 * 1024


def _cparams(sem):
    return pltpu.CompilerParams(dimension_semantics=sem, vmem_limit_bytes=VMEM_LIMIT)


def _diff_lambda_init(layer):
    return 0.8 - 0.6 * math.exp(-0.3 * layer)


def _ada_kernel(s_ref, w_ref, b_ref, o_ref):
    s = s_ref[...]
    s = s * (1.0 / (1.0 + jnp.exp(-s)))
    o_ref[0] = jnp.dot(s, w_ref[0], preferred_element_type=F32,
                       precision=lax.Precision.HIGHEST) + b_ref[0]


def _ada_all(cond, ada_w, ada_b):
    depth, d, n = ada_w.shape
    return pl.pallas_call(
        _ada_kernel,
        grid=(depth, n // ADA_COL_TILE),
        in_specs=[
            pl.BlockSpec((8, d), lambda i, j: (0, 0)),
            pl.BlockSpec((1, d, ADA_COL_TILE), lambda i, j: (i, 0, j)),
            pl.BlockSpec((1, 1, ADA_COL_TILE), lambda i, j: (i, 0, j)),
        ],
        out_specs=pl.BlockSpec((1, 8, ADA_COL_TILE), lambda i, j: (i, 0, j)),
        out_shape=jax.ShapeDtypeStruct((depth, 8, n), F32),
        compiler_params=_cparams(("parallel", "parallel")),
        name="ada",
    )(cond, ada_w, ada_b.reshape(depth, 1, n))


def _norm_mod(x, g, shift, scale):
    ms = jnp.mean(x * x, axis=-1, keepdims=True)
    y = x * lax.rsqrt(ms + EPS) * g
    return y * (1.0 + scale) + shift


def _rope(a, cos, sin_lo, sin_hi):
    up = pltpu.roll(a, HEAD_W - ROPE_QUARTER, 1)
    dn = pltpu.roll(a, ROPE_QUARTER, 1)
    return a * cos + up * sin_lo + dn * sin_hi


def _rope_tables(seq, pad_rows):
    t = jnp.arange(seq)
    row = (t // GRID_W).astype(F32)[:, None]
    col = (t % GRID_W).astype(F32)[:, None]
    inv = ROPE_THETA ** (-jnp.arange(ROPE_QUARTER, dtype=F32) / ROPE_QUARTER)
    ang = jnp.concatenate([row * inv, row * inv, col * inv, col * inv], axis=1)
    cos, sin = jnp.cos(ang), jnp.sin(ang)
    first = (jnp.arange(4 * ROPE_QUARTER) % (2 * ROPE_QUARTER)) < ROPE_QUARTER
    sin_lo = jnp.where(first, -sin, 0.0)
    sin_hi = jnp.where(first, 0.0, sin)

    def finish(tab, fill):
        tab = jnp.concatenate([tab, tab], axis=1)
        return jnp.concatenate([tab, jnp.full((pad_rows, HEAD_W), fill, F32)], axis=0)

    return finish(cos, 1.0), finish(sin_lo, 0.0), finish(sin_hi, 0.0)


def _qkv_kernel(x_ref, g_ref, mod_ref, w_ref, cos_ref, slo_ref, shi_ref, o_ref, h_ref, *, rope_tiles, q_tiles):
    j = pl.program_id(1)

    @pl.when(j == 0)
    def _():
        h = _norm_mod(x_ref[...], g_ref[...], mod_ref[0, 0:1, :], mod_ref[0, 1:2, :])
        h_ref[...] = h.astype(BF16)

    acc = jnp.dot(h_ref[...], w_ref[...], preferred_element_type=F32)

    @pl.when(j < rope_tiles)
    def _():
        sc = jnp.where(j < q_tiles, DIFF_SCALE, 1.0).astype(F32)
        cos, slo, shi = cos_ref[...], slo_ref[...], shi_ref[...]
        for c in range(acc.shape[1] // HEAD_W):
            a = acc[:, c * HEAD_W:(c + 1) * HEAD_W]
            o_ref[:, c * HEAD_W:(c + 1) * HEAD_W] = (_rope(a, cos, slo, shi) * sc).astype(BF16)

    @pl.when(j >= rope_tiles)
    def _():
        o_ref[...] = acc.astype(BF16)


def _qkv_proj(xa, g, mod, w, tabs, *, tiles_per_batch, n_lat_tiles, n_batch):
    rows, d = xa.shape
    n = w.shape[1]
    tm, tn = ROW_TILE, QKV_COL_TILE
    row_spec = lambda shape: pl.BlockSpec(shape, lambda i, j: (i, 0))
    tab_spec = pl.BlockSpec(
        (tm, HEAD_W), lambda i, j: (jnp.where(i < n_lat_tiles, i % tiles_per_batch, tiles_per_batch), 0))
    kern = functools.partial(_qkv_kernel, rope_tiles=2 * d // tn, q_tiles=d // tn)
    return pl.pallas_call(
        kern,
        grid=(rows // tm, n // tn),
        in_specs=[
            row_spec((tm, d)),
            pl.BlockSpec((1, d), lambda i, j: (0, 0)),
            pl.BlockSpec((1, 6, d), lambda i, j: (jnp.minimum(i // tiles_per_batch, n_batch), 0, 0)),
            pl.BlockSpec((d, tn), lambda i, j: (0, j)),
            tab_spec, tab_spec, tab_spec,
        ],
        out_specs=pl.BlockSpec((tm, tn), lambda i, j: (i, j)),
        out_shape=jax.ShapeDtypeStruct((rows, n), BF16),
        scratch_shapes=[pltpu.VMEM((tm, d), BF16)],
        compiler_params=_cparams(("parallel", "arbitrary")),
        name="diff_qkv",
    )(xa, g, mod, w, *tabs)


def _mla_proj_kernel(x_ref, g_ref, mod_ref, wd_ref, qg_ref, kvg_ref, wuq_ref, wukn_ref, wuv_ref,
                     cos_ref, slo_ref, shi_ref, q_ref, k_ref, v_ref):
    h = _norm_mod(x_ref[...], g_ref[...], mod_ref[0, 0:1, :], mod_ref[0, 1:2, :]).astype(BF16)
    a = jnp.dot(h, wd_ref[...], preferred_element_type=F32)
    cq = a[:, :512]
    ckv = a[:, 512:1024]
    kpe = a[:, 1024:1152]
    cqn = cq * lax.rsqrt(jnp.sum(cq * cq, axis=-1, keepdims=True) * (1.0 / MLA_Q_LORA) + EPS) * qg_ref[...]
    ckvn = ckv * lax.rsqrt(jnp.mean(ckv * ckv, axis=-1, keepdims=True) + EPS) * kvg_ref[...]
    cqb = (cqn * MLA_SCALE).astype(BF16)
    ckvb = ckvn.astype(BF16)
    cos, slo, shi = cos_ref[...], slo_ref[...], shi_ref[...]

    kpe_r = _rope(kpe, cos, slo, shi).astype(BF16)
    group = 4
    for hg in range(HEADS // group):
        qc = jnp.dot(cqb, wuq_ref[:, hg * group * MLA_QK_W:(hg + 1) * group * MLA_QK_W],
                     preferred_element_type=F32)
        kc = jnp.dot(ckvb, wukn_ref[:, hg * group * HEAD_W:(hg + 1) * group * HEAD_W],
                     preferred_element_type=F32)
        for hh in range(group):
            o = (hg * group + hh) * MLA_QK_W
            q_ref[:, o:o + HEAD_W] = qc[:, hh * MLA_QK_W:hh * MLA_QK_W + HEAD_W].astype(BF16)
            qp = qc[:, hh * MLA_QK_W + HEAD_W:(hh + 1) * MLA_QK_W]
            q_ref[:, o + HEAD_W:o + MLA_QK_W] = _rope(qp, cos, slo, shi).astype(BF16)
            k_ref[:, o:o + HEAD_W] = kc[:, hh * HEAD_W:(hh + 1) * HEAD_W].astype(BF16)
            k_ref[:, o + HEAD_W:o + MLA_QK_W] = kpe_r
    v_ref[...] = jnp.dot(ckvb, wuv_ref[...], preferred_element_type=F32).astype(BF16)


def _mla_proj(xa, g, mod, wd, qg, kvg, wuq, wukn, wuv, tabs, *, seq, n_batch):
    rows, d = xa.shape
    tm = MLA_ROW_TILE
    tiles_per_batch = seq // tm
    n_lat_tiles = n_batch * tiles_per_batch
    full = lambda arr: pl.BlockSpec(arr.shape, lambda i: (0, 0))
    tab_spec = pl.BlockSpec(
        (tm, HEAD_W), lambda i: (jnp.where(i < n_lat_tiles, i % tiles_per_batch, tiles_per_batch), 0))
    return pl.pallas_call(
        _mla_proj_kernel,
        grid=(rows // tm,),
        in_specs=[
            pl.BlockSpec((tm, d), lambda i: (i, 0)),
            full(g),
            pl.BlockSpec((1, 6, d), lambda i: (jnp.minimum(i // tiles_per_batch, n_batch), 0, 0)),
            full(wd), full(qg), full(kvg), full(wuq), full(wukn), full(wuv),
            tab_spec, tab_spec, tab_spec,
        ],
        out_specs=[
            pl.BlockSpec((tm, HEADS * MLA_QK_W), lambda i: (i, 0)),
            pl.BlockSpec((tm, HEADS * MLA_QK_W), lambda i: (i, 0)),
            pl.BlockSpec((tm, HEADS * HEAD_W), lambda i: (i, 0)),
        ],
        out_shape=[
            jax.ShapeDtypeStruct((rows, HEADS * MLA_QK_W), BF16),
            jax.ShapeDtypeStruct((rows, HEADS * MLA_QK_W), BF16),
            jax.ShapeDtypeStruct((rows, HEADS * HEAD_W), BF16),
        ],
        compiler_params=_cparams(("parallel",)),
        name="mla_proj",
    )(xa, g, mod, wd, qg, kvg, wuq, wukn, wuv, *tabs)


def _attn_kernel(*refs, diff, has_lat, key_tile, lam_init):
    refs = list(refs)
    q_ref, kc_ref, vc_ref = refs[:3]
    pos = 3
    if has_lat:
        kl_ref, vl_ref = refs[pos:pos + 2]
        pos += 2
    if diff:
        lam_ref, sg_ref = refs[pos:pos + 2]
        pos += 2
    o_ref, m_ref, l_ref, acc_ref = refs[pos:pos + 4]

    q = q_ref[...]
    tq = q.shape[0]
    if diff:
        lane = lax.broadcasted_iota(jnp.int32, q.shape, 1)
        zero = jnp.zeros_like(q)
        q = jnp.concatenate([jnp.where(lane < DIFF_HEAD_DIM, q, zero),
                             jnp.where(lane >= DIFF_HEAD_DIM, q, zero)], axis=0)

    def scores(k):
        return lax.dot_general(k, q, (((1,), (1,)), ((), ())), preferred_element_type=F32)

    def weighted(v, p):
        return lax.dot_general(v, p.astype(BF16), (((0,), (0,)), ((), ())),
                               preferred_element_type=F32)

    s = scores(kc_ref[...])
    m0 = jnp.max(s, axis=0, keepdims=True)
    p = jnp.exp(s - m0)
    m_ref[...] = m0
    l_ref[...] = jnp.sum(p, axis=0, keepdims=True)
    acc_ref[...] = weighted(vc_ref[...], p)

    if has_lat:
        def body(i, carry):
            start = pl.multiple_of(i * key_tile, key_tile)
            s = scores(kl_ref[pl.ds(start, key_tile), :])
            m_prev = m_ref[...]
            m_new = jnp.maximum(m_prev, jnp.max(s, axis=0, keepdims=True))
            alpha = jnp.exp(m_prev - m_new)
            p = jnp.exp(s - m_new)
            l_ref[...] = alpha * l_ref[...] + jnp.sum(p, axis=0, keepdims=True)
            acc_ref[...] = alpha * acc_ref[...] + weighted(vl_ref[pl.ds(start, key_tile), :], p)
            m_ref[...] = m_new
            return carry

        lax.fori_loop(0, kl_ref.shape[0] // key_tile, body, 0)

    acc = acc_ref[...]
    inv_l = 1.0 / l_ref[...]
    if diff:
        lf = lam_ref[...]
        lam = (jnp.exp(jnp.sum(lf[0:1] * lf[1:2], axis=-1, keepdims=True))
               - jnp.exp(jnp.sum(lf[2:3] * lf[3:4], axis=-1, keepdims=True)) + lam_init)
        o = acc[:, :tq] * inv_l[:, :tq] - lam * (acc[:, tq:] * inv_l[:, tq:])
        ms = jnp.mean(o * o, axis=0, keepdims=True)
        o = o * lax.rsqrt(ms + EPS) * sg_ref[...] * (1.0 - lam_init)
    else:
        o = acc * inv_l
    o_ref[...] = o.T.astype(BF16)


def _attention(q_arr, k_arr, v_arr, *, diff, seq, n_batch, n_ctx, q_col, k_col, v_col, qk_w,
               lat_queries, lam=None, subln_g=None, lam_init=0.0, prev_out=None):
    rows = q_arr.shape[0]
    ctx_row0 = n_batch * seq
    if lat_queries:
        tq = DIFF_Q_TILE if diff else MLA_Q_TILE
        n_q = seq // tq
        q_row = lambda b, qi: b * n_q + qi
    else:
        tq = n_ctx
        n_q = 1
        q_row = lambda b, qi: ctx_row0 // n_ctx + b
    nq_cols = 2 * tq if diff else tq

    in_specs = [
        pl.BlockSpec((tq, qk_w), lambda b, h, qi: (q_row(b, qi), q_col + h)),
        pl.BlockSpec((n_ctx, qk_w), lambda b, h, qi: (ctx_row0 // n_ctx + b, k_col + h)),
        pl.BlockSpec((n_ctx, HEAD_W), lambda b, h, qi: (ctx_row0 // n_ctx + b, v_col + h)),
    ]
    args = [q_arr, k_arr, v_arr]
    if lat_queries:
        in_specs += [
            pl.BlockSpec((seq, qk_w), lambda b, h, qi: (b, k_col + h)),
            pl.BlockSpec((seq, HEAD_W), lambda b, h, qi: (b, v_col + h)),
        ]
        args += [k_arr, v_arr]
    if diff:
        in_specs += [
            pl.BlockSpec(lam.shape, lambda b, h, qi: (0, 0)),
            pl.BlockSpec(subln_g.shape, lambda b, h, qi: (0, 0)),
        ]
        args += [lam, subln_g]
    aliases = {}
    if prev_out is not None:
        in_specs.append(pl.BlockSpec(memory_space=pl.ANY))
        args.append(prev_out)
        aliases = {len(args) - 1: 0}

    kern = functools.partial(_attn_kernel, diff=diff, has_lat=lat_queries, key_tile=KEY_TILE, lam_init=lam_init)
    if prev_out is not None:
        inner = kern
        kern = lambda *refs: inner(*refs[:len(args) - 1], *refs[len(args):])

    return pl.pallas_call(
        kern,
        grid=(n_batch, HEADS, n_q),
        in_specs=in_specs,
        out_specs=pl.BlockSpec((tq, HEAD_W), lambda b, h, qi: (q_row(b, qi), h)),
        out_shape=jax.ShapeDtypeStruct((rows, HEADS * HEAD_W), BF16),
        scratch_shapes=[
            pltpu.VMEM((1, nq_cols), F32),
            pltpu.VMEM((1, nq_cols), F32),
            pltpu.VMEM((HEAD_W, nq_cols), F32),
        ],
        input_output_aliases=aliases,
        compiler_params=_cparams(("parallel", "parallel", "arbitrary")),
        name=("diff" if diff else "mla") + ("_attn_lat" if lat_queries else "_attn_ctx"),
    )(*args)


def _oproj_kernel(o_ref, w_ref, x_ref, mod_ref, y_ref):
    y_ref[...] = x_ref[...] + mod_ref[0, 2:3, :] * jnp.dot(o_ref[...], w_ref[...], preferred_element_type=F32)


def _oproj(o, w, xa, mod, *, n_tiles, tiles_per_batch, n_batch):
    d = xa.shape[1]
    tm = ROW_TILE
    return pl.pallas_call(
        _oproj_kernel,
        grid=(n_tiles,),
        in_specs=[
            pl.BlockSpec((tm, o.shape[1]), lambda i: (i, 0)),
            pl.BlockSpec(w.shape, lambda i: (0, 0)),
            pl.BlockSpec((tm, d), lambda i: (i, 0)),
            pl.BlockSpec((1, 6, d), lambda i: (jnp.minimum(i // tiles_per_batch, n_batch), 0, 0)),
        ],
        out_specs=pl.BlockSpec((tm, d), lambda i: (i, 0)),
        out_shape=jax.ShapeDtypeStruct((n_tiles * tm, d), F32),
        compiler_params=_cparams(("parallel",)),
        name="oproj",
    )(o, w, xa, mod)


def _mlp_kernel(x_ref, g_ref, mod_ref, w1_ref, w2_ref, fg_ref, y_ref, h_ref, acc_ref, *, final):
    j = pl.program_id(1)

    @pl.when(j == 0)
    def _():
        h = _norm_mod(x_ref[...], g_ref[...], mod_ref[0, 3:4, :], mod_ref[0, 4:5, :])
        h_ref[...] = h.astype(BF16)

    a = jnp.dot(h_ref[...], w1_ref[...], preferred_element_type=F32)
    a = jnp.maximum(a, 0.0)
    part = jnp.dot((a * a).astype(BF16), w2_ref[...], preferred_element_type=F32)

    @pl.when(j == 0)
    def _():
        acc_ref[...] = part

    @pl.when(j > 0)
    def _():
        acc_ref[...] += part

    @pl.when(j == pl.num_programs(1) - 1)
    def _():
        y = x_ref[...] + mod_ref[0, 5:6, :] * acc_ref[...]
        if final:
            ms = jnp.mean(y * y, axis=-1, keepdims=True)
            y = y * lax.rsqrt(ms + EPS) * fg_ref[...]
        y_ref[...] = y


def _mlp(xa, g, mod, w1, w2, final_g, *, n_tiles, tiles_per_batch, n_batch, final):
    d = xa.shape[1]
    ff = w1.shape[1]
    tm, tf = ROW_TILE, FF_TILE
    return pl.pallas_call(
        functools.partial(_mlp_kernel, final=final),
        grid=(n_tiles, ff // tf),
        in_specs=[
            pl.BlockSpec((tm, d), lambda i, j: (i, 0)),
            pl.BlockSpec((1, d), lambda i, j: (0, 0)),
            pl.BlockSpec((1, 6, d), lambda i, j: (jnp.minimum(i // tiles_per_batch, n_batch), 0, 0)),
            pl.BlockSpec((d, tf), lambda i, j: (0, j)),
            pl.BlockSpec((tf, d), lambda i, j: (j, 0)),
            pl.BlockSpec((1, d), lambda i, j: (0, 0)),
        ],
        out_specs=pl.BlockSpec((tm, d), lambda i, j: (i, 0)),
        out_shape=jax.ShapeDtypeStruct((n_tiles * tm, d), F32),
        scratch_shapes=[pltpu.VMEM((tm, d), BF16), pltpu.VMEM((tm, d), F32)],
        compiler_params=_cparams(("parallel", "arbitrary")),
        name="mlp",
    )(xa, g, mod, w1, w2, final_g)


def _mla_weights(wdown, q_norm_g, wuq, kv_norm_g, wukv):
    d = wdown.shape[0]
    qpad = 512 - MLA_Q_LORA
    z = lambda n: jnp.zeros((d, n), wdown.dtype)
    kv_end = MLA_Q_LORA + MLA_KV_LORA
    wd = jnp.concatenate([wdown[:, :MLA_Q_LORA], z(qpad), wdown[:, MLA_Q_LORA:kv_end],
                          wdown[:, kv_end:], z(HEAD_W - MLA_ROPE)], axis=1).astype(BF16)
    qg = jnp.pad(q_norm_g, (0, qpad)).reshape(1, -1)
    kvg = kv_norm_g.reshape(1, -1)
    wq = wuq.reshape(MLA_Q_LORA, HEADS, MLA_NOPE + MLA_ROPE)
    wq = jnp.pad(wq, ((0, qpad), (0, 0), (0, MLA_QK_W - MLA_NOPE - MLA_ROPE)))
    wq = wq.reshape(512, HEADS * MLA_QK_W).astype(BF16)
    wkv = wukv.reshape(MLA_KV_LORA, HEADS, 2 * HEAD_W)
    wukn = wkv[:, :, :HEAD_W].reshape(MLA_KV_LORA, HEADS * HEAD_W).astype(BF16)
    wuv = wkv[:, :, HEAD_W:].reshape(MLA_KV_LORA, HEADS * HEAD_W).astype(BF16)
    return wd, qg, kvg, wq, wukn, wuv


def kernel(x, c, ctx, c_ctx, ada_w, ada_b, norm_mix_g, norm_mlp_g, dif_wqkv, dif_wo, dif_lambda, dif_subln_g,
           mla_wdown, mla_q_norm_g, mla_wuq, mla_kv_norm_g, mla_wukv, mla_wo, mlp_w1, mlp_w2, final_g):
    n_batch, seq, d = x.shape
    n_ctx = ctx.shape[1]
    depth = ada_w.shape[0]
    assert d == HEADS * HEAD_W and seq % ROW_TILE == 0 and (n_batch * n_ctx) % ROW_TILE == 0
    assert seq % KEY_TILE == 0 and n_batch + 1 <= 8 and MLA_ROW_TILE == n_ctx

    tiles_per_batch = seq // ROW_TILE
    n_lat_tiles = n_batch * tiles_per_batch
    n_all_tiles = n_lat_tiles + n_batch * n_ctx // ROW_TILE
    geom = dict(tiles_per_batch=tiles_per_batch, n_batch=n_batch)

    cond = jnp.concatenate([c, c_ctx[None, :], jnp.zeros((8 - n_batch - 1, d), F32)], axis=0)
    mods = _ada_all(cond, ada_w, ada_b)[:, :n_batch + 1].reshape(depth, n_batch + 1, 6, d)

    tabs = _rope_tables(seq, ROW_TILE)
    xa = jnp.concatenate([x.reshape(n_batch * seq, d), ctx.reshape(n_batch * n_ctx, d)], axis=0)
    row = lambda v: v.reshape(1, -1)

    for i in range(depth):
        last = i == depth - 1
        j = i // N_MIXERS
        mod = mods[i]
        att = dict(seq=seq, n_batch=n_batch, n_ctx=n_ctx)
        if i % N_MIXERS == 0:
            qkv = _qkv_proj(xa, row(norm_mix_g[i]), mod, dif_wqkv[j].astype(BF16), tabs,
                            n_lat_tiles=n_lat_tiles, **geom)
            att.update(diff=True, q_col=0, k_col=HEADS, v_col=2 * HEADS, qk_w=HEAD_W, lam=dif_lambda[j],
                       subln_g=dif_subln_g[j].reshape(-1, 1), lam_init=_diff_lambda_init(i))
            o = _attention(qkv, qkv, qkv, lat_queries=True, **att)
            if not last:
                o = _attention(qkv, qkv, qkv, lat_queries=False, prev_out=o, **att)
            wo = dif_wo[j]
        else:
            wts = _mla_weights(mla_wdown[j], mla_q_norm_g[j], mla_wuq[j], mla_kv_norm_g[j], mla_wukv[j])
            qa, ka, va = _mla_proj(xa, row(norm_mix_g[i]), mod, *wts, tabs, seq=seq, n_batch=n_batch)
            att.update(diff=False, q_col=0, k_col=0, v_col=0, qk_w=MLA_QK_W)
            o = _attention(qa, ka, va, lat_queries=True, **att)
            if not last:
                o = _attention(qa, ka, va, lat_queries=False, prev_out=o, **att)
            wo = mla_wo[j]
        n_tiles = n_lat_tiles if last else n_all_tiles
        xa = _oproj(o, wo.astype(BF16), xa, mod, n_tiles=n_tiles, **geom)
        xa = _mlp(xa, row(norm_mlp_g[i]), mod, mlp_w1[i].astype(BF16), mlp_w2[i].astype(BF16), row(final_g),
                  n_tiles=n_tiles, final=last, **geom)

    return xa.reshape(n_batch, seq, d)
```

```python
import functools
import math

import jax
import jax.numpy as jnp
from jax import lax
from jax.experimental import pallas as pl
from jax.experimental.pallas import tpu as pltpu

F32 = jnp.float32
BF16 = jnp.bfloat16

GRID_W = 64
N_MIXERS = 2
HEADS = 16
HEAD_W = 128
DIFF_HEAD_DIM = 64
MLA_NOPE = 128
MLA_ROPE = 64
MLA_Q_LORA = 448
MLA_KV_LORA = 512
MLA_QK_W = 256
DIFF_SCALE = DIFF_HEAD_DIM ** -0.5
MLA_SCALE = (MLA_NOPE + MLA_ROPE) ** -0.5
ROPE_THETA = 10000.0
ROPE_QUARTER = 16
EPS = 1e-6
LOG2E = math.log2(math.e)
SUM_ROWS = 16

ROW_TILE = 512
MLA_ROW_TILE = 256
ADA_COL_TILE = 1024
QKV_COL_TILE = 512
FF_TILE = 1024
DIFF_Q_TILE = 512
MLA_Q_TILE = 1024
KEY_TILE = 512
VMEM_LIMIT = 56 * 1024 * 1024


def _cparams(sem):
    return pltpu.CompilerParams(dimension_semantics=sem, vmem_limit_bytes=VMEM_LIMIT)


def _diff_lambda_init(layer):
    return 0.8 - 0.6 * math.exp(-0.3 * layer)


def _ada_kernel(s_ref, w_ref, b_ref, o_ref):
    s = s_ref[...]
    s = s * (1.0 / (1.0 + jnp.exp(-s)))
    o_ref[0] = jnp.dot(s, w_ref[0], preferred_element_type=F32,
                       precision=lax.Precision.HIGHEST) + b_ref[0]


def _ada_all(cond, ada_w, ada_b):
    depth, d, n = ada_w.shape
    return pl.pallas_call(
        _ada_kernel,
        grid=(depth, n // ADA_COL_TILE),
        in_specs=[
            pl.BlockSpec((8, d), lambda i, j: (0, 0)),
            pl.BlockSpec((1, d, ADA_COL_TILE), lambda i, j: (i, 0, j)),
            pl.BlockSpec((1, 1, ADA_COL_TILE), lambda i, j: (i, 0, j)),
        ],
        out_specs=pl.BlockSpec((1, 8, ADA_COL_TILE), lambda i, j: (i, 0, j)),
        out_shape=jax.ShapeDtypeStruct((depth, 8, n), F32),
        compiler_params=_cparams(("parallel", "parallel")),
        name="ada",
    )(cond, ada_w, ada_b.reshape(depth, 1, n))


def _norm_mod(x, g, shift, scale):
    ms = jnp.mean(x * x, axis=-1, keepdims=True)
    y = x * lax.rsqrt(ms + EPS) * g
    return y * (1.0 + scale) + shift


def _rope(a, cos, sin_lo, sin_hi):
    up = pltpu.roll(a, HEAD_W - ROPE_QUARTER, 1)
    dn = pltpu.roll(a, ROPE_QUARTER, 1)
    return a * cos + up * sin_lo + dn * sin_hi


def _rope_tables(seq, pad_rows):
    t = jnp.arange(seq)
    row = (t // GRID_W).astype(F32)[:, None]
    col = (t % GRID_W).astype(F32)[:, None]
    inv = ROPE_THETA ** (-jnp.arange(ROPE_QUARTER, dtype=F32) / ROPE_QUARTER)
    ang = jnp.concatenate([row * inv, row * inv, col * inv, col * inv], axis=1)
    cos, sin = jnp.cos(ang), jnp.sin(ang)
    first = (jnp.arange(4 * ROPE_QUARTER) % (2 * ROPE_QUARTER)) < ROPE_QUARTER
    sin_lo = jnp.where(first, -sin, 0.0)
    sin_hi = jnp.where(first, 0.0, sin)

    def finish(tab, fill):
        tab = jnp.concatenate([tab, tab], axis=1)
        return jnp.concatenate([tab, jnp.full((pad_rows, HEAD_W), fill, F32)], axis=0)

    return finish(cos, 1.0), finish(sin_lo, 0.0), finish(sin_hi, 0.0)


def _qkv_kernel(x_ref, g_ref, mod_ref, w_ref, cos_ref, slo_ref, shi_ref, o_ref, h_ref, *, rope_tiles, q_tiles):
    j = pl.program_id(1)

    @pl.when(j == 0)
    def _():
        h = _norm_mod(x_ref[...], g_ref[...], mod_ref[0, 0:1, :], mod_ref[0, 1:2, :])
        h_ref[...] = h.astype(BF16)

    acc = jnp.dot(h_ref[...], w_ref[...], preferred_element_type=F32)

    @pl.when(j < rope_tiles)
    def _():
        sc = jnp.where(j < q_tiles, DIFF_SCALE * LOG2E, 1.0).astype(F32)
        cos, slo, shi = cos_ref[...], slo_ref[...], shi_ref[...]
        for c in range(acc.shape[1] // HEAD_W):
            a = acc[:, c * HEAD_W:(c + 1) * HEAD_W]
            o_ref[:, c * HEAD_W:(c + 1) * HEAD_W] = (_rope(a, cos, slo, shi) * sc).astype(BF16)

    @pl.when(j >= rope_tiles)
    def _():
        o_ref[...] = acc.astype(BF16)


def _qkv_proj(xa, g, mod, w, tabs, *, tiles_per_batch, n_lat_tiles, n_batch):
    rows, d = xa.shape
    n = w.shape[1]
    tm, tn = ROW_TILE, QKV_COL_TILE
    row_spec = lambda shape: pl.BlockSpec(shape, lambda i, j: (i, 0))
    tab_spec = pl.BlockSpec(
        (tm, HEAD_W), lambda i, j: (jnp.where(i < n_lat_tiles, i % tiles_per_batch, tiles_per_batch), 0))
    kern = functools.partial(_qkv_kernel, rope_tiles=2 * d // tn, q_tiles=d // tn)
    return pl.pallas_call(
        kern,
        grid=(rows // tm, n // tn),
        in_specs=[
            row_spec((tm, d)),
            pl.BlockSpec((1, d), lambda i, j: (0, 0)),
            pl.BlockSpec((1, 6, d), lambda i, j: (jnp.minimum(i // tiles_per_batch, n_batch), 0, 0)),
            pl.BlockSpec((d, tn), lambda i, j: (0, j)),
            tab_spec, tab_spec, tab_spec,
        ],
        out_specs=pl.BlockSpec((tm, tn), lambda i, j: (i, j)),
        out_shape=jax.ShapeDtypeStruct((rows, n), BF16),
        scratch_shapes=[pltpu.VMEM((tm, d), BF16)],
        compiler_params=_cparams(("parallel", "arbitrary")),
        name="diff_qkv",
    )(xa, g, mod, w, *tabs)


def _mla_proj_kernel(x_ref, g_ref, mod_ref, wd_ref, qg_ref, kvg_ref, wuq_ref, wukn_ref, wuv_ref,
                     cos_ref, slo_ref, shi_ref, q_ref, k_ref, v_ref):
    h = _norm_mod(x_ref[...], g_ref[...], mod_ref[0, 0:1, :], mod_ref[0, 1:2, :]).astype(BF16)
    a = jnp.dot(h, wd_ref[...], preferred_element_type=F32)
    cq = a[:, :512]
    ckv = a[:, 512:1024]
    kpe = a[:, 1024:1152]
    cqn = cq * lax.rsqrt(jnp.sum(cq * cq, axis=-1, keepdims=True) * (1.0 / MLA_Q_LORA) + EPS) * qg_ref[...]
    ckvn = ckv * lax.rsqrt(jnp.mean(ckv * ckv, axis=-1, keepdims=True) + EPS) * kvg_ref[...]
    cqb = (cqn * (MLA_SCALE * LOG2E)).astype(BF16)
    ckvb = ckvn.astype(BF16)
    cos, slo, shi = cos_ref[...], slo_ref[...], shi_ref[...]

    kpe_r = _rope(kpe, cos, slo, shi).astype(BF16)
    group = 4
    for hg in range(HEADS // group):
        qc = jnp.dot(cqb, wuq_ref[:, hg * group * MLA_QK_W:(hg + 1) * group * MLA_QK_W],
                     preferred_element_type=F32)
        kc = jnp.dot(ckvb, wukn_ref[:, hg * group * HEAD_W:(hg + 1) * group * HEAD_W],
                     preferred_element_type=F32)
        for hh in range(group):
            o = (hg * group + hh) * MLA_QK_W
            q_ref[:, o:o + HEAD_W] = qc[:, hh * MLA_QK_W:hh * MLA_QK_W + HEAD_W].astype(BF16)
            qp = qc[:, hh * MLA_QK_W + HEAD_W:(hh + 1) * MLA_QK_W]
            q_ref[:, o + HEAD_W:o + MLA_QK_W] = _rope(qp, cos, slo, shi).astype(BF16)
            k_ref[:, o:o + HEAD_W] = kc[:, hh * HEAD_W:(hh + 1) * HEAD_W].astype(BF16)
            k_ref[:, o + HEAD_W:o + MLA_QK_W] = kpe_r
    v_ref[...] = jnp.dot(ckvb, wuv_ref[...], preferred_element_type=F32).astype(BF16)


def _mla_proj(xa, g, mod, wd, qg, kvg, wuq, wukn, wuv, tabs, *, seq, n_batch):
    rows, d = xa.shape
    tm = MLA_ROW_TILE
    tiles_per_batch = seq // tm
    n_lat_tiles = n_batch * tiles_per_batch
    full = lambda arr: pl.BlockSpec(arr.shape, lambda i: (0, 0))
    tab_spec = pl.BlockSpec(
        (tm, HEAD_W), lambda i: (jnp.where(i < n_lat_tiles, i % tiles_per_batch, tiles_per_batch), 0))
    return pl.pallas_call(
        _mla_proj_kernel,
        grid=(rows // tm,),
        in_specs=[
            pl.BlockSpec((tm, d), lambda i: (i, 0)),
            full(g),
            pl.BlockSpec((1, 6, d), lambda i: (jnp.minimum(i // tiles_per_batch, n_batch), 0, 0)),
            full(wd), full(qg), full(kvg), full(wuq), full(wukn), full(wuv),
            tab_spec, tab_spec, tab_spec,
        ],
        out_specs=[
            pl.BlockSpec((tm, HEADS * MLA_QK_W), lambda i: (i, 0)),
            pl.BlockSpec((tm, HEADS * MLA_QK_W), lambda i: (i, 0)),
            pl.BlockSpec((tm, HEADS * HEAD_W), lambda i: (i, 0)),
        ],
        out_shape=[
            jax.ShapeDtypeStruct((rows, HEADS * MLA_QK_W), BF16),
            jax.ShapeDtypeStruct((rows, HEADS * MLA_QK_W), BF16),
            jax.ShapeDtypeStruct((rows, HEADS * HEAD_W), BF16),
        ],
        compiler_params=_cparams(("parallel",)),
        name="mla_proj",
    )(xa, g, mod, wd, qg, kvg, wuq, wukn, wuv, *tabs)


def _attn_kernel(*refs, diff, has_lat, key_tile, lam_init):
    refs = list(refs)
    q_ref, kc_ref, vc_ref = refs[:3]
    pos = 3
    if has_lat:
        kl_ref, vl_ref = refs[pos:pos + 2]
        pos += 2
    if diff:
        lam_ref, sg_ref = refs[pos:pos + 2]
        pos += 2
    o_ref, m_ref, acc_ref = refs[pos:pos + 3]
    pos += 3
    if has_lat:
        s_bufs, p_bufs, a_bufs, c_bufs = (refs[pos + 2 * i:pos + 2 * i + 2] for i in range(4))

    q = q_ref[...]
    tq = q.shape[0]
    if diff:
        lane = lax.broadcasted_iota(jnp.int32, q.shape, 1)
        zero = jnp.zeros_like(q)
        q = jnp.concatenate([jnp.where(lane < DIFF_HEAD_DIM, q, zero),
                             jnp.where(lane >= DIFF_HEAD_DIM, q, zero)], axis=0)

    def scores(k):
        return lax.dot_general(k, q, (((1,), (1,)), ((), ())), preferred_element_type=F32)

    def weighted(vt, p):
        lhs = jnp.concatenate([vt, jnp.ones((SUM_ROWS, vt.shape[1]), BF16)], axis=0)
        return jnp.dot(lhs, p, preferred_element_type=F32)

    s = scores(kc_ref[...])
    m0 = jnp.max(s, axis=0, keepdims=True)
    m_ref[...] = m0
    acc_ref[...] = weighted(vc_ref[...], jnp.exp2(s - m0).astype(BF16))

    if has_lat:
        n_chunks = kl_ref.shape[0] // key_tile

        def chunk(t):
            return pl.ds(pl.multiple_of(t * key_tile, key_tile), key_tile)

        def qk(t, slot):
            s = scores(kl_ref[chunk(t), :])
            s_bufs[slot][...] = s
            c_bufs[slot][...] = jnp.max(s, axis=0, keepdims=True)

        def sm(slot):
            m_prev = m_ref[...]
            m_new = jnp.maximum(m_prev, c_bufs[slot][...])
            a_bufs[slot][...] = jnp.exp2(m_prev - m_new)
            p_bufs[slot][...] = jnp.exp2(s_bufs[slot][...] - m_new).astype(BF16)
            m_ref[...] = m_new

        def pv(t, slot):
            acc_ref[...] = a_bufs[slot][...] * acc_ref[...] + weighted(vl_ref[:, chunk(t)], p_bufs[slot][...])

        qk(0, 0)
        sm(0)
        qk(1, 1)

        def body(j, carry):
            t = 2 * j + 1
            pv(t - 1, 0)
            sm(1)
            qk(t + 1, 0)
            pv(t, 1)
            sm(0)
            qk(t + 2, 1)
            return carry

        lax.fori_loop(0, (n_chunks - 2) // 2, body, 0)
        pv(n_chunks - 2, 0)
        sm(1)
        pv(n_chunks - 1, 1)

    acc = acc_ref[...]
    inv_l = 1.0 / acc[HEAD_W:HEAD_W + 1, :]
    acc = acc[:HEAD_W, :]
    if diff:
        lf = lam_ref[...]
        lam = (jnp.exp(jnp.sum(lf[0:1] * lf[1:2], axis=-1, keepdims=True))
               - jnp.exp(jnp.sum(lf[2:3] * lf[3:4], axis=-1, keepdims=True)) + lam_init)
        o = acc[:, :tq] * inv_l[:, :tq] - lam * (acc[:, tq:] * inv_l[:, tq:])
        ms = jnp.mean(o * o, axis=0, keepdims=True)
        o = o * lax.rsqrt(ms + EPS) * sg_ref[...] * (1.0 - lam_init)
    else:
        o = acc * inv_l
    o_ref[...] = o.T.astype(BF16)


def _attention(q_arr, k_arr, vt_arr, *, diff, seq, n_batch, n_ctx, q_col, k_col, qk_w,
               lat_queries, lam=None, subln_g=None, lam_init=0.0, prev_out=None):
    rows = q_arr.shape[0]
    ctx_row0 = n_batch * seq
    if lat_queries:
        tq = DIFF_Q_TILE if diff else MLA_Q_TILE
        n_q = seq // tq
        q_row = lambda b, qi: b * n_q + qi
    else:
        tq = n_ctx
        n_q = 1
        q_row = lambda b, qi: ctx_row0 // n_ctx + b
    nq_cols = 2 * tq if diff else tq

    in_specs = [
        pl.BlockSpec((tq, qk_w), lambda b, h, qi: (q_row(b, qi), q_col + h)),
        pl.BlockSpec((n_ctx, qk_w), lambda b, h, qi: (ctx_row0 // n_ctx + b, k_col + h)),
        pl.BlockSpec((HEAD_W, n_ctx), lambda b, h, qi: (h, ctx_row0 // n_ctx + b)),
    ]
    args = [q_arr, k_arr, vt_arr]
    scratch = [pltpu.VMEM((1, nq_cols), F32), pltpu.VMEM((HEAD_W + SUM_ROWS, nq_cols), F32)]
    if lat_queries:
        in_specs += [
            pl.BlockSpec((seq, qk_w), lambda b, h, qi: (b, k_col + h)),
            pl.BlockSpec((HEAD_W, seq), lambda b, h, qi: (h, b)),
        ]
        args += [k_arr, vt_arr]
        scratch += ([pltpu.VMEM((KEY_TILE, nq_cols), F32)] * 2 + [pltpu.VMEM((KEY_TILE, nq_cols), BF16)] * 2
                    + [pltpu.VMEM((1, nq_cols), F32)] * 4)
    if diff:
        in_specs += [
            pl.BlockSpec(lam.shape, lambda b, h, qi: (0, 0)),
            pl.BlockSpec(subln_g.shape, lambda b, h, qi: (0, 0)),
        ]
        args += [lam, subln_g]
    aliases = {}
    if prev_out is not None:
        in_specs.append(pl.BlockSpec(memory_space=pl.ANY))
        args.append(prev_out)
        aliases = {len(args) - 1: 0}

    kern = functools.partial(_attn_kernel, diff=diff, has_lat=lat_queries, key_tile=KEY_TILE, lam_init=lam_init)
    if prev_out is not None:
        inner = kern
        kern = lambda *refs: inner(*refs[:len(args) - 1], *refs[len(args):])

    return pl.pallas_call(
        kern,
        grid=(n_batch, HEADS, n_q),
        in_specs=in_specs,
        out_specs=pl.BlockSpec((tq, HEAD_W), lambda b, h, qi: (q_row(b, qi), h)),
        out_shape=jax.ShapeDtypeStruct((rows, HEADS * HEAD_W), BF16),
        scratch_shapes=scratch,
        input_output_aliases=aliases,
        compiler_params=_cparams(("parallel", "parallel", "arbitrary")),
        name=("diff" if diff else "mla") + ("_attn_lat" if lat_queries else "_attn_ctx"),
    )(*args)


def _oproj_kernel(o_ref, w_ref, x_ref, mod_ref, y_ref):
    y_ref[...] = x_ref[...] + mod_ref[0, 2:3, :] * jnp.dot(o_ref[...], w_ref[...], preferred_element_type=F32)


def _oproj(o, w, xa, mod, *, n_tiles, tiles_per_batch, n_batch):
    d = xa.shape[1]
    tm = ROW_TILE
    return pl.pallas_call(
        _oproj_kernel,
        grid=(n_tiles,),
        in_specs=[
            pl.BlockSpec((tm, o.shape[1]), lambda i: (i, 0)),
            pl.BlockSpec(w.shape, lambda i: (0, 0)),
            pl.BlockSpec((tm, d), lambda i: (i, 0)),
            pl.BlockSpec((1, 6, d), lambda i: (jnp.minimum(i // tiles_per_batch, n_batch), 0, 0)),
        ],
        out_specs=pl.BlockSpec((tm, d), lambda i: (i, 0)),
        out_shape=jax.ShapeDtypeStruct((n_tiles * tm, d), F32),
        compiler_params=_cparams(("parallel",)),
        name="oproj",
    )(o, w, xa, mod)


def _mlp_kernel(x_ref, g_ref, mod_ref, w1_ref, w2_ref, fg_ref, y_ref, h_ref, acc_ref, *, final):
    j = pl.program_id(1)

    @pl.when(j == 0)
    def _():
        h = _norm_mod(x_ref[...], g_ref[...], mod_ref[0, 3:4, :], mod_ref[0, 4:5, :])
        h_ref[...] = h.astype(BF16)

    a = jnp.dot(h_ref[...], w1_ref[...], preferred_element_type=F32)
    a = jnp.maximum(a, 0.0)
    part = jnp.dot((a * a).astype(BF16), w2_ref[...], preferred_element_type=F32)

    @pl.when(j == 0)
    def _():
        acc_ref[...] = part

    @pl.when(j > 0)
    def _():
        acc_ref[...] += part

    @pl.when(j == pl.num_programs(1) - 1)
    def _():
        y = x_ref[...] + mod_ref[0, 5:6, :] * acc_ref[...]
        if final:
            ms = jnp.mean(y * y, axis=-1, keepdims=True)
            y = y * lax.rsqrt(ms + EPS) * fg_ref[...]
        y_ref[...] = y


def _mlp(xa, g, mod, w1, w2, final_g, *, n_tiles, tiles_per_batch, n_batch, final):
    d = xa.shape[1]
    ff = w1.shape[1]
    tm, tf = ROW_TILE, FF_TILE
    return pl.pallas_call(
        functools.partial(_mlp_kernel, final=final),
        grid=(n_tiles, ff // tf),
        in_specs=[
            pl.BlockSpec((tm, d), lambda i, j: (i, 0)),
            pl.BlockSpec((1, d), lambda i, j: (0, 0)),
            pl.BlockSpec((1, 6, d), lambda i, j: (jnp.minimum(i // tiles_per_batch, n_batch), 0, 0)),
            pl.BlockSpec((d, tf), lambda i, j: (0, j)),
            pl.BlockSpec((tf, d), lambda i, j: (j, 0)),
            pl.BlockSpec((1, d), lambda i, j: (0, 0)),
        ],
        out_specs=pl.BlockSpec((tm, d), lambda i, j: (i, 0)),
        out_shape=jax.ShapeDtypeStruct((n_tiles * tm, d), F32),
        scratch_shapes=[pltpu.VMEM((tm, d), BF16), pltpu.VMEM((tm, d), F32)],
        compiler_params=_cparams(("parallel", "arbitrary")),
        name="mlp",
    )(xa, g, mod, w1, w2, final_g)


def _mla_weights(wdown, q_norm_g, wuq, kv_norm_g, wukv):
    d = wdown.shape[0]
    qpad = 512 - MLA_Q_LORA
    z = lambda n: jnp.zeros((d, n), wdown.dtype)
    kv_end = MLA_Q_LORA + MLA_KV_LORA
    wd = jnp.concatenate([wdown[:, :MLA_Q_LORA], z(qpad), wdown[:, MLA_Q_LORA:kv_end],
                          wdown[:, kv_end:], z(HEAD_W - MLA_ROPE)], axis=1).astype(BF16)
    qg = jnp.pad(q_norm_g, (0, qpad)).reshape(1, -1)
    kvg = kv_norm_g.reshape(1, -1)
    wq = wuq.reshape(MLA_Q_LORA, HEADS, MLA_NOPE + MLA_ROPE)
    wq = jnp.pad(wq, ((0, qpad), (0, 0), (0, MLA_QK_W - MLA_NOPE - MLA_ROPE)))
    wq = wq.reshape(512, HEADS * MLA_QK_W).astype(BF16)
    wkv = wukv.reshape(MLA_KV_LORA, HEADS, 2 * HEAD_W)
    wukn = wkv[:, :, :HEAD_W].reshape(MLA_KV_LORA, HEADS * HEAD_W).astype(BF16)
    wuv = wkv[:, :, HEAD_W:].reshape(MLA_KV_LORA, HEADS * HEAD_W).astype(BF16)
    return wd, qg, kvg, wq, wukn, wuv


def kernel(x, c, ctx, c_ctx, ada_w, ada_b, norm_mix_g, norm_mlp_g, dif_wqkv, dif_wo, dif_lambda, dif_subln_g,
           mla_wdown, mla_q_norm_g, mla_wuq, mla_kv_norm_g, mla_wukv, mla_wo, mlp_w1, mlp_w2, final_g):
    n_batch, seq, d = x.shape
    n_ctx = ctx.shape[1]
    depth = ada_w.shape[0]
    assert d == HEADS * HEAD_W and seq % ROW_TILE == 0 and (n_batch * n_ctx) % ROW_TILE == 0
    assert seq % (2 * KEY_TILE) == 0 and n_batch + 1 <= 8 and MLA_ROW_TILE == n_ctx

    tiles_per_batch = seq // ROW_TILE
    n_lat_tiles = n_batch * tiles_per_batch
    n_all_tiles = n_lat_tiles + n_batch * n_ctx // ROW_TILE
    geom = dict(tiles_per_batch=tiles_per_batch, n_batch=n_batch)

    cond = jnp.concatenate([c, c_ctx[None, :], jnp.zeros((8 - n_batch - 1, d), F32)], axis=0)
    mods = _ada_all(cond, ada_w, ada_b)[:, :n_batch + 1].reshape(depth, n_batch + 1, 6, d)

    tabs = _rope_tables(seq, ROW_TILE)
    xa = jnp.concatenate([x.reshape(n_batch * seq, d), ctx.reshape(n_batch * n_ctx, d)], axis=0)
    row = lambda v: v.reshape(1, -1)

    for i in range(depth):
        last = i == depth - 1
        j = i // N_MIXERS
        mod = mods[i]
        att = dict(seq=seq, n_batch=n_batch, n_ctx=n_ctx)
        if i % N_MIXERS == 0:
            qkv = _qkv_proj(xa, row(norm_mix_g[i]), mod, dif_wqkv[j].astype(BF16), tabs,
                            n_lat_tiles=n_lat_tiles, **geom)
            att.update(diff=True, q_col=0, k_col=HEADS, qk_w=HEAD_W, lam=dif_lambda[j],
                       subln_g=dif_subln_g[j].reshape(-1, 1), lam_init=_diff_lambda_init(i))
            vt = qkv[:, 2 * d:].T
            o = _attention(qkv, qkv, vt, lat_queries=True, **att)
            if not last:
                o = _attention(qkv, qkv, vt, lat_queries=False, prev_out=o, **att)
            wo = dif_wo[j]
        else:
            wts = _mla_weights(mla_wdown[j], mla_q_norm_g[j], mla_wuq[j], mla_kv_norm_g[j], mla_wukv[j])
            qa, ka, va = _mla_proj(xa, row(norm_mix_g[i]), mod, *wts, tabs, seq=seq, n_batch=n_batch)
            att.update(diff=False, q_col=0, k_col=0, qk_w=MLA_QK_W)
            vt = va.T
            o = _attention(qa, ka, vt, lat_queries=True, **att)
            if not last:
                o = _attention(qa, ka, vt, lat_queries=False, prev_out=o, **att)
            wo = mla_wo[j]
        n_tiles = n_lat_tiles if last else n_all_tiles
        xa = _oproj(o, wo.astype(BF16), xa, mod, n_tiles=n_tiles, **geom)
        xa = _mlp(xa, row(norm_mlp_g[i]), mod, mlp_w1[i].astype(BF16), mlp_w2[i].astype(BF16), row(final_g),
                  n_tiles=n_tiles, final=last, **geom)

    return xa.reshape(n_batch, seq, d)
```

```python
import functools
import math

import jax
import jax.numpy as jnp
from jax import lax
from jax.experimental import pallas as pl
from jax.experimental.pallas import tpu as pltpu

F32 = jnp.float32
BF16 = jnp.bfloat16

GRID_W = 64
N_MIXERS = 2
HEADS = 16
HEAD_W = 128
DIFF_HEAD_DIM = 64
MLA_NOPE = 128
MLA_ROPE = 64
MLA_Q_LORA = 448
MLA_KV_LORA = 512
MLA_QK_W = 256
DIFF_SCALE = DIFF_HEAD_DIM ** -0.5
MLA_SCALE = (MLA_NOPE + MLA_ROPE) ** -0.5
ROPE_THETA = 10000.0
ROPE_QUARTER = 16
EPS = 1e-6
LOG2E = math.log2(math.e)
SUM_ROWS = 16

ROW_TILE = 512
MLA_ROW_TILE = 256
ADA_COL_TILE = 1024
QKV_COL_TILE = 512
FF_TILE = 1024
DIFF_Q_TILE = 1024
MLA_Q_TILE = 2048
KEY_TILE = 512
VMEM_LIMIT = 56 * 1024 * 1024


def _cparams(sem):
    return pltpu.CompilerParams(dimension_semantics=sem, vmem_limit_bytes=VMEM_LIMIT)


def _diff_lambda_init(layer):
    return 0.8 - 0.6 * math.exp(-0.3 * layer)


def _ada_kernel(s_ref, w_ref, b_ref, o_ref):
    s = s_ref[...]
    s = s * (1.0 / (1.0 + jnp.exp(-s)))
    o_ref[0] = jnp.dot(s, w_ref[0], preferred_element_type=F32,
                       precision=lax.Precision.HIGHEST) + b_ref[0]


def _ada_all(cond, ada_w, ada_b):
    depth, d, n = ada_w.shape
    return pl.pallas_call(
        _ada_kernel,
        grid=(depth, n // ADA_COL_TILE),
        in_specs=[
            pl.BlockSpec((8, d), lambda i, j: (0, 0)),
            pl.BlockSpec((1, d, ADA_COL_TILE), lambda i, j: (i, 0, j)),
            pl.BlockSpec((1, 1, ADA_COL_TILE), lambda i, j: (i, 0, j)),
        ],
        out_specs=pl.BlockSpec((1, 8, ADA_COL_TILE), lambda i, j: (i, 0, j)),
        out_shape=jax.ShapeDtypeStruct((depth, 8, n), F32),
        compiler_params=_cparams(("parallel", "parallel")),
        name="ada",
    )(cond, ada_w, ada_b.reshape(depth, 1, n))


def _norm_mod(x, g, shift, scale):
    ms = jnp.mean(x * x, axis=-1, keepdims=True)
    y = x * lax.rsqrt(ms + EPS) * g
    return y * (1.0 + scale) + shift


def _rope(a, cos, sin_lo, sin_hi):
    up = pltpu.roll(a, HEAD_W - ROPE_QUARTER, 1)
    dn = pltpu.roll(a, ROPE_QUARTER, 1)
    return a * cos + up * sin_lo + dn * sin_hi


def _rope_tables(seq, pad_rows):
    t = jnp.arange(seq)
    row = (t // GRID_W).astype(F32)[:, None]
    col = (t % GRID_W).astype(F32)[:, None]
    inv = ROPE_THETA ** (-jnp.arange(ROPE_QUARTER, dtype=F32) / ROPE_QUARTER)
    ang = jnp.concatenate([row * inv, row * inv, col * inv, col * inv], axis=1)
    cos, sin = jnp.cos(ang), jnp.sin(ang)
    first = (jnp.arange(4 * ROPE_QUARTER) % (2 * ROPE_QUARTER)) < ROPE_QUARTER
    sin_lo = jnp.where(first, -sin, 0.0)
    sin_hi = jnp.where(first, 0.0, sin)

    def finish(tab, fill):
        tab = jnp.concatenate([tab, tab], axis=1)
        return jnp.concatenate([tab, jnp.full((pad_rows, HEAD_W), fill, F32)], axis=0)

    return finish(cos, 1.0), finish(sin_lo, 0.0), finish(sin_hi, 0.0)


def _qkv_kernel(x_ref, g_ref, mod_ref, w_ref, cos_ref, slo_ref, shi_ref, o_ref, h_ref, *, rope_tiles, q_tiles):
    j = pl.program_id(1)

    @pl.when(j == 0)
    def _():
        h = _norm_mod(x_ref[...], g_ref[...], mod_ref[0, 0:1, :], mod_ref[0, 1:2, :])
        h_ref[...] = h.astype(BF16)

    acc = jnp.dot(h_ref[...], w_ref[...], preferred_element_type=F32)

    @pl.when(j < rope_tiles)
    def _():
        sc = jnp.where(j < q_tiles, DIFF_SCALE * LOG2E, 1.0).astype(F32)
        cos, slo, shi = cos_ref[...], slo_ref[...], shi_ref[...]
        for c in range(acc.shape[1] // HEAD_W):
            a = acc[:, c * HEAD_W:(c + 1) * HEAD_W]
            o_ref[:, c * HEAD_W:(c + 1) * HEAD_W] = (_rope(a, cos, slo, shi) * sc).astype(BF16)

    @pl.when(j >= rope_tiles)
    def _():
        o_ref[...] = acc.astype(BF16)


def _qkv_proj(xa, g, mod, w, tabs, *, tiles_per_batch, n_lat_tiles, n_batch):
    rows, d = xa.shape
    n = w.shape[1]
    tm, tn = ROW_TILE, QKV_COL_TILE
    row_spec = lambda shape: pl.BlockSpec(shape, lambda i, j: (i, 0))
    tab_spec = pl.BlockSpec(
        (tm, HEAD_W), lambda i, j: (jnp.where(i < n_lat_tiles, i % tiles_per_batch, tiles_per_batch), 0))
    kern = functools.partial(_qkv_kernel, rope_tiles=2 * d // tn, q_tiles=d // tn)
    return pl.pallas_call(
        kern,
        grid=(rows // tm, n // tn),
        in_specs=[
            row_spec((tm, d)),
            pl.BlockSpec((1, d), lambda i, j: (0, 0)),
            pl.BlockSpec((1, 6, d), lambda i, j: (jnp.minimum(i // tiles_per_batch, n_batch), 0, 0)),
            pl.BlockSpec((d, tn), lambda i, j: (0, j)),
            tab_spec, tab_spec, tab_spec,
        ],
        out_specs=pl.BlockSpec((tm, tn), lambda i, j: (i, j)),
        out_shape=jax.ShapeDtypeStruct((rows, n), BF16),
        scratch_shapes=[pltpu.VMEM((tm, d), BF16)],
        compiler_params=_cparams(("parallel", "arbitrary")),
        name="diff_qkv",
    )(xa, g, mod, w, *tabs)


def _mla_proj_kernel(x_ref, g_ref, mod_ref, wd_ref, qg_ref, kvg_ref, wuq_ref, wukn_ref, wuv_ref,
                     cos_ref, slo_ref, shi_ref, q_ref, k_ref, v_ref):
    h = _norm_mod(x_ref[...], g_ref[...], mod_ref[0, 0:1, :], mod_ref[0, 1:2, :]).astype(BF16)
    a = jnp.dot(h, wd_ref[...], preferred_element_type=F32)
    cq = a[:, :512]
    ckv = a[:, 512:1024]
    kpe = a[:, 1024:1152]
    cqn = cq * lax.rsqrt(jnp.sum(cq * cq, axis=-1, keepdims=True) * (1.0 / MLA_Q_LORA) + EPS) * qg_ref[...]
    ckvn = ckv * lax.rsqrt(jnp.mean(ckv * ckv, axis=-1, keepdims=True) + EPS) * kvg_ref[...]
    cqb = (cqn * (MLA_SCALE * LOG2E)).astype(BF16)
    ckvb = ckvn.astype(BF16)
    cos, slo, shi = cos_ref[...], slo_ref[...], shi_ref[...]

    kpe_r = _rope(kpe, cos, slo, shi).astype(BF16)
    group = 4
    for hg in range(HEADS // group):
        qc = jnp.dot(cqb, wuq_ref[:, hg * group * MLA_QK_W:(hg + 1) * group * MLA_QK_W],
                     preferred_element_type=F32)
        kc = jnp.dot(ckvb, wukn_ref[:, hg * group * HEAD_W:(hg + 1) * group * HEAD_W],
                     preferred_element_type=F32)
        for hh in range(group):
            o = (hg * group + hh) * MLA_QK_W
            q_ref[:, o:o + HEAD_W] = qc[:, hh * MLA_QK_W:hh * MLA_QK_W + HEAD_W].astype(BF16)
            qp = qc[:, hh * MLA_QK_W + HEAD_W:(hh + 1) * MLA_QK_W]
            q_ref[:, o + HEAD_W:o + MLA_QK_W] = _rope(qp, cos, slo, shi).astype(BF16)
            k_ref[:, o:o + HEAD_W] = kc[:, hh * HEAD_W:(hh + 1) * HEAD_W].astype(BF16)
            k_ref[:, o + HEAD_W:o + MLA_QK_W] = kpe_r
    v_ref[...] = jnp.dot(ckvb, wuv_ref[...], preferred_element_type=F32).astype(BF16)


def _mla_proj(xa, g, mod, wd, qg, kvg, wuq, wukn, wuv, tabs, *, seq, n_batch):
    rows, d = xa.shape
    tm = MLA_ROW_TILE
    tiles_per_batch = seq // tm
    n_lat_tiles = n_batch * tiles_per_batch
    full = lambda arr: pl.BlockSpec(arr.shape, lambda i: (0, 0))
    tab_spec = pl.BlockSpec(
        (tm, HEAD_W), lambda i: (jnp.where(i < n_lat_tiles, i % tiles_per_batch, tiles_per_batch), 0))
    return pl.pallas_call(
        _mla_proj_kernel,
        grid=(rows // tm,),
        in_specs=[
            pl.BlockSpec((tm, d), lambda i: (i, 0)),
            full(g),
            pl.BlockSpec((1, 6, d), lambda i: (jnp.minimum(i // tiles_per_batch, n_batch), 0, 0)),
            full(wd), full(qg), full(kvg), full(wuq), full(wukn), full(wuv),
            tab_spec, tab_spec, tab_spec,
        ],
        out_specs=[
            pl.BlockSpec((tm, HEADS * MLA_QK_W), lambda i: (i, 0)),
            pl.BlockSpec((tm, HEADS * MLA_QK_W), lambda i: (i, 0)),
            pl.BlockSpec((tm, HEADS * HEAD_W), lambda i: (i, 0)),
        ],
        out_shape=[
            jax.ShapeDtypeStruct((rows, HEADS * MLA_QK_W), BF16),
            jax.ShapeDtypeStruct((rows, HEADS * MLA_QK_W), BF16),
            jax.ShapeDtypeStruct((rows, HEADS * HEAD_W), BF16),
        ],
        compiler_params=_cparams(("parallel",)),
        name="mla_proj",
    )(xa, g, mod, wd, qg, kvg, wuq, wukn, wuv, *tabs)


def _attn_kernel(*refs, diff, has_lat, key_tile, lam_init):
    refs = list(refs)
    q_ref, kc_ref, vc_ref = refs[:3]
    pos = 3
    if has_lat:
        kl_ref, vl_ref = refs[pos:pos + 2]
        pos += 2
    if diff:
        lam_ref, sg_ref = refs[pos:pos + 2]
        pos += 2
    o_ref, m_ref, acc_ref = refs[pos:pos + 3]
    pos += 3
    if has_lat:
        s_ref, p_ref, a_ref, c_ref = refs[pos:pos + 4]

    q = q_ref[...]
    tq = q.shape[0]
    if diff:
        lane = lax.broadcasted_iota(jnp.int32, q.shape, 1)
        zero = jnp.zeros_like(q)
        q = jnp.concatenate([jnp.where(lane < DIFF_HEAD_DIM, q, zero),
                             jnp.where(lane >= DIFF_HEAD_DIM, q, zero)], axis=0)

    def scores(k):
        return lax.dot_general(k, q, (((1,), (1,)), ((), ())), preferred_element_type=F32)

    def weighted(vt, p):
        lhs = jnp.concatenate([vt, jnp.ones((SUM_ROWS, vt.shape[1]), BF16)], axis=0)
        return jnp.dot(lhs, p, preferred_element_type=F32)

    s = scores(kc_ref[...])
    m0 = jnp.max(s, axis=0, keepdims=True)
    m_ref[...] = m0
    acc_ref[...] = weighted(vc_ref[...], jnp.exp2(s - m0).astype(BF16))

    if has_lat:
        n_chunks = kl_ref.shape[0] // key_tile

        def chunk(t):
            return pl.ds(pl.multiple_of(t * key_tile, key_tile), key_tile)

        def qk(t):
            s = scores(kl_ref[chunk(t), :])
            s_ref[...] = s
            c_ref[...] = jnp.max(s, axis=0, keepdims=True)

        def sm():
            m_prev = m_ref[...]
            m_new = jnp.maximum(m_prev, c_ref[...])
            a_ref[...] = jnp.exp2(m_prev - m_new)
            p_ref[...] = jnp.exp2(s_ref[...] - m_new).astype(BF16)
            m_ref[...] = m_new

        def pv(t):
            acc_ref[...] = a_ref[...] * acc_ref[...] + weighted(vl_ref[:, chunk(t)], p_ref[...])

        qk(0)
        sm()
        qk(1)

        def body(t, carry):
            pv(t - 1)
            sm()
            qk(t + 1)
            return carry

        lax.fori_loop(1, n_chunks - 1, body, 0, unroll=2)
        pv(n_chunks - 2)
        sm()
        pv(n_chunks - 1)

    acc = acc_ref[...]
    inv_l = 1.0 / acc[HEAD_W:HEAD_W + 1, :]
    acc = acc[:HEAD_W, :]
    if diff:
        lf = lam_ref[...]
        lam = (jnp.exp(jnp.sum(lf[0:1] * lf[1:2], axis=-1, keepdims=True))
               - jnp.exp(jnp.sum(lf[2:3] * lf[3:4], axis=-1, keepdims=True)) + lam_init)
        o = acc[:, :tq] * inv_l[:, :tq] - lam * (acc[:, tq:] * inv_l[:, tq:])
        ms = jnp.mean(o * o, axis=0, keepdims=True)
        o = o * lax.rsqrt(ms + EPS) * sg_ref[...] * (1.0 - lam_init)
    else:
        o = acc * inv_l
    o_ref[...] = o.T.astype(BF16)


def _attention(q_arr, k_arr, vt_arr, *, diff, seq, n_batch, n_ctx, q_col, k_col, qk_w,
               lat_queries, lam=None, subln_g=None, lam_init=0.0, prev_out=None):
    rows = q_arr.shape[0]
    ctx_row0 = n_batch * seq
    if lat_queries:
        tq = DIFF_Q_TILE if diff else MLA_Q_TILE
        n_q = seq // tq
        q_row = lambda b, qi: b * n_q + qi
    else:
        tq = n_ctx
        n_q = 1
        q_row = lambda b, qi: ctx_row0 // n_ctx + b
    nq_cols = 2 * tq if diff else tq

    in_specs = [
        pl.BlockSpec((tq, qk_w), lambda b, h, qi: (q_row(b, qi), q_col + h)),
        pl.BlockSpec((n_ctx, qk_w), lambda b, h, qi: (ctx_row0 // n_ctx + b, k_col + h)),
        pl.BlockSpec((HEAD_W, n_ctx), lambda b, h, qi: (h, ctx_row0 // n_ctx + b)),
    ]
    args = [q_arr, k_arr, vt_arr]
    scratch = [pltpu.VMEM((1, nq_cols), F32), pltpu.VMEM((HEAD_W + SUM_ROWS, nq_cols), F32)]
    if lat_queries:
        in_specs += [
            pl.BlockSpec((seq, qk_w), lambda b, h, qi: (b, k_col + h)),
            pl.BlockSpec((HEAD_W, seq), lambda b, h, qi: (h, b)),
        ]
        args += [k_arr, vt_arr]
        scratch += [pltpu.VMEM((KEY_TILE, nq_cols), F32), pltpu.VMEM((KEY_TILE, nq_cols), BF16),
                    pltpu.VMEM((1, nq_cols), F32), pltpu.VMEM((1, nq_cols), F32)]
    if diff:
        in_specs += [
            pl.BlockSpec(lam.shape, lambda b, h, qi: (0, 0)),
            pl.BlockSpec(subln_g.shape, lambda b, h, qi: (0, 0)),
        ]
        args += [lam, subln_g]
    aliases = {}
    if prev_out is not None:
        in_specs.append(pl.BlockSpec(memory_space=pl.ANY))
        args.append(prev_out)
        aliases = {len(args) - 1: 0}

    kern = functools.partial(_attn_kernel, diff=diff, has_lat=lat_queries, key_tile=KEY_TILE, lam_init=lam_init)
    if prev_out is not None:
        inner = kern
        kern = lambda *refs: inner(*refs[:len(args) - 1], *refs[len(args):])

    return pl.pallas_call(
        kern,
        grid=(n_batch, HEADS, n_q),
        in_specs=in_specs,
        out_specs=pl.BlockSpec((tq, HEAD_W), lambda b, h, qi: (q_row(b, qi), h)),
        out_shape=jax.ShapeDtypeStruct((rows, HEADS * HEAD_W), BF16),
        scratch_shapes=scratch,
        input_output_aliases=aliases,
        compiler_params=_cparams(("parallel", "parallel", "arbitrary")),
        name=("diff" if diff else "mla") + ("_attn_lat" if lat_queries else "_attn_ctx"),
    )(*args)


def _oproj_kernel(o_ref, w_ref, x_ref, mod_ref, y_ref):
    y_ref[...] = x_ref[...] + mod_ref[0, 2:3, :] * jnp.dot(o_ref[...], w_ref[...], preferred_element_type=F32)


def _oproj(o, w, xa, mod, *, n_tiles, tiles_per_batch, n_batch):
    d = xa.shape[1]
    tm = ROW_TILE
    return pl.pallas_call(
        _oproj_kernel,
        grid=(n_tiles,),
        in_specs=[
            pl.BlockSpec((tm, o.shape[1]), lambda i: (i, 0)),
            pl.BlockSpec(w.shape, lambda i: (0, 0)),
            pl.BlockSpec((tm, d), lambda i: (i, 0)),
            pl.BlockSpec((1, 6, d), lambda i: (jnp.minimum(i // tiles_per_batch, n_batch), 0, 0)),
        ],
        out_specs=pl.BlockSpec((tm, d), lambda i: (i, 0)),
        out_shape=jax.ShapeDtypeStruct((n_tiles * tm, d), F32),
        compiler_params=_cparams(("parallel",)),
        name="oproj",
    )(o, w, xa, mod)


def _mlp_kernel(x_ref, g_ref, mod_ref, w1_ref, w2_ref, fg_ref, y_ref, h_ref, acc_ref, *, final):
    j = pl.program_id(1)

    @pl.when(j == 0)
    def _():
        h = _norm_mod(x_ref[...], g_ref[...], mod_ref[0, 3:4, :], mod_ref[0, 4:5, :])
        h_ref[...] = h.astype(BF16)

    a = jnp.dot(h_ref[...], w1_ref[...], preferred_element_type=F32)
    a = jnp.maximum(a, 0.0)
    part = jnp.dot((a * a).astype(BF16), w2_ref[...], preferred_element_type=F32)

    @pl.when(j == 0)
    def _():
        acc_ref[...] = part

    @pl.when(j > 0)
    def _():
        acc_ref[...] += part

    @pl.when(j == pl.num_programs(1) - 1)
    def _():
        y = x_ref[...] + mod_ref[0, 5:6, :] * acc_ref[...]
        if final:
            ms = jnp.mean(y * y, axis=-1, keepdims=True)
            y = y * lax.rsqrt(ms + EPS) * fg_ref[...]
        y_ref[...] = y


def _mlp(xa, g, mod, w1, w2, final_g, *, n_tiles, tiles_per_batch, n_batch, final):
    d = xa.shape[1]
    ff = w1.shape[1]
    tm, tf = ROW_TILE, FF_TILE
    return pl.pallas_call(
        functools.partial(_mlp_kernel, final=final),
        grid=(n_tiles, ff // tf),
        in_specs=[
            pl.BlockSpec((tm, d), lambda i, j: (i, 0)),
            pl.BlockSpec((1, d), lambda i, j: (0, 0)),
            pl.BlockSpec((1, 6, d), lambda i, j: (jnp.minimum(i // tiles_per_batch, n_batch), 0, 0)),
            pl.BlockSpec((d, tf), lambda i, j: (0, j)),
            pl.BlockSpec((tf, d), lambda i, j: (j, 0)),
            pl.BlockSpec((1, d), lambda i, j: (0, 0)),
        ],
        out_specs=pl.BlockSpec((tm, d), lambda i, j: (i, 0)),
        out_shape=jax.ShapeDtypeStruct((n_tiles * tm, d), F32),
        scratch_shapes=[pltpu.VMEM((tm, d), BF16), pltpu.VMEM((tm, d), F32)],
        compiler_params=_cparams(("parallel", "arbitrary")),
        name="mlp",
    )(xa, g, mod, w1, w2, final_g)


def _mla_weights(wdown, q_norm_g, wuq, kv_norm_g, wukv):
    d = wdown.shape[0]
    qpad = 512 - MLA_Q_LORA
    z = lambda n: jnp.zeros((d, n), wdown.dtype)
    kv_end = MLA_Q_LORA + MLA_KV_LORA
    wd = jnp.concatenate([wdown[:, :MLA_Q_LORA], z(qpad), wdown[:, MLA_Q_LORA:kv_end],
                          wdown[:, kv_end:], z(HEAD_W - MLA_ROPE)], axis=1).astype(BF16)
    qg = jnp.pad(q_norm_g, (0, qpad)).reshape(1, -1)
    kvg = kv_norm_g.reshape(1, -1)
    wq = wuq.reshape(MLA_Q_LORA, HEADS, MLA_NOPE + MLA_ROPE)
    wq = jnp.pad(wq, ((0, qpad), (0, 0), (0, MLA_QK_W - MLA_NOPE - MLA_ROPE)))
    wq = wq.reshape(512, HEADS * MLA_QK_W).astype(BF16)
    wkv = wukv.reshape(MLA_KV_LORA, HEADS, 2 * HEAD_W)
    wukn = wkv[:, :, :HEAD_W].reshape(MLA_KV_LORA, HEADS * HEAD_W).astype(BF16)
    wuv = wkv[:, :, HEAD_W:].reshape(MLA_KV_LORA, HEADS * HEAD_W).astype(BF16)
    return wd, qg, kvg, wq, wukn, wuv


def kernel(x, c, ctx, c_ctx, ada_w, ada_b, norm_mix_g, norm_mlp_g, dif_wqkv, dif_wo, dif_lambda, dif_subln_g,
           mla_wdown, mla_q_norm_g, mla_wuq, mla_kv_norm_g, mla_wukv, mla_wo, mlp_w1, mlp_w2, final_g):
    n_batch, seq, d = x.shape
    n_ctx = ctx.shape[1]
    depth = ada_w.shape[0]
    assert d == HEADS * HEAD_W and seq % ROW_TILE == 0 and (n_batch * n_ctx) % ROW_TILE == 0
    assert seq % (2 * KEY_TILE) == 0 and n_batch + 1 <= 8 and MLA_ROW_TILE == n_ctx

    tiles_per_batch = seq // ROW_TILE
    n_lat_tiles = n_batch * tiles_per_batch
    n_all_tiles = n_lat_tiles + n_batch * n_ctx // ROW_TILE
    geom = dict(tiles_per_batch=tiles_per_batch, n_batch=n_batch)

    cond = jnp.concatenate([c, c_ctx[None, :], jnp.zeros((8 - n_batch - 1, d), F32)], axis=0)
    mods = _ada_all(cond, ada_w, ada_b)[:, :n_batch + 1].reshape(depth, n_batch + 1, 6, d)

    tabs = _rope_tables(seq, ROW_TILE)
    xa = jnp.concatenate([x.reshape(n_batch * seq, d), ctx.reshape(n_batch * n_ctx, d)], axis=0)
    row = lambda v: v.reshape(1, -1)

    for i in range(depth):
        last = i == depth - 1
        j = i // N_MIXERS
        mod = mods[i]
        att = dict(seq=seq, n_batch=n_batch, n_ctx=n_ctx)
        if i % N_MIXERS == 0:
            qkv = _qkv_proj(xa, row(norm_mix_g[i]), mod, dif_wqkv[j].astype(BF16), tabs,
                            n_lat_tiles=n_lat_tiles, **geom)
            att.update(diff=True, q_col=0, k_col=HEADS, qk_w=HEAD_W, lam=dif_lambda[j],
                       subln_g=dif_subln_g[j].reshape(-1, 1), lam_init=_diff_lambda_init(i))
            vt = qkv[:, 2 * d:].T
            o = _attention(qkv, qkv, vt, lat_queries=True, **att)
            if not last:
                o = _attention(qkv, qkv, vt, lat_queries=False, prev_out=o, **att)
            wo = dif_wo[j]
        else:
            wts = _mla_weights(mla_wdown[j], mla_q_norm_g[j], mla_wuq[j], mla_kv_norm_g[j], mla_wukv[j])
            qa, ka, va = _mla_proj(xa, row(norm_mix_g[i]), mod, *wts, tabs, seq=seq, n_batch=n_batch)
            att.update(diff=False, q_col=0, k_col=0, qk_w=MLA_QK_W)
            vt = va.T
            o = _attention(qa, ka, vt, lat_queries=True, **att)
            if not last:
                o = _attention(qa, ka, vt, lat_queries=False, prev_out=o, **att)
            wo = mla_wo[j]
        n_tiles = n_lat_tiles if last else n_all_tiles
        xa = _oproj(o, wo.astype(BF16), xa, mod, n_tiles=n_tiles, **geom)
        xa = _mlp(xa, row(norm_mlp_g[i]), mod, mlp_w1[i].astype(BF16), mlp_w2[i].astype(BF16), row(final_g),
                  n_tiles=n_tiles, final=last, **geom)

    return xa.reshape(n_batch, seq, d)
```

```python
import functools
import math

import jax
import jax.numpy as jnp
from jax import lax
from jax.experimental import pallas as pl
from jax.experimental.pallas import tpu as pltpu

F32 = jnp.float32
BF16 = jnp.bfloat16

GRID_W = 64
N_MIXERS = 2
HEADS = 16
HEAD_W = 128
DIFF_HEAD_DIM = 64
MLA_NOPE = 128
MLA_ROPE = 64
MLA_Q_LORA = 448
MLA_KV_LORA = 512
MLA_QK_W = 256
DIFF_SCALE = DIFF_HEAD_DIM ** -0.5
MLA_SCALE = (MLA_NOPE + MLA_ROPE) ** -0.5
ROPE_THETA = 10000.0
ROPE_QUARTER = 16
EPS = 1e-6
LOG2E = math.log2(math.e)
SUM_ROWS = 16

ROW_TILE = 512
MLA_ROW_TILE = 256
ADA_COL_TILE = 1024
QKV_COL_TILE = 1024
FF_TILE = 1024
DIFF_Q_TILE = 1024
MLA_Q_TILE = 2048
KEY_TILE = 512
VMEM_LIMIT = 56 * 1024 * 1024


def _cparams(sem):
    return pltpu.CompilerParams(dimension_semantics=sem, vmem_limit_bytes=VMEM_LIMIT)


def _diff_lambda_init(layer):
    return 0.8 - 0.6 * math.exp(-0.3 * layer)


def _ada_kernel(s_ref, w_ref, b_ref, o_ref):
    s = s_ref[...]
    s = s * (1.0 / (1.0 + jnp.exp(-s)))
    o_ref[0] = jnp.dot(s, w_ref[0], preferred_element_type=F32,
                       precision=lax.Precision.HIGHEST) + b_ref[0]


def _ada_all(cond, ada_w, ada_b):
    depth, d, n = ada_w.shape
    return pl.pallas_call(
        _ada_kernel,
        grid=(depth, n // ADA_COL_TILE),
        in_specs=[
            pl.BlockSpec((8, d), lambda i, j: (0, 0)),
            pl.BlockSpec((1, d, ADA_COL_TILE), lambda i, j: (i, 0, j)),
            pl.BlockSpec((1, 1, ADA_COL_TILE), lambda i, j: (i, 0, j)),
        ],
        out_specs=pl.BlockSpec((1, 8, ADA_COL_TILE), lambda i, j: (i, 0, j)),
        out_shape=jax.ShapeDtypeStruct((depth, 8, n), F32),
        compiler_params=_cparams(("parallel", "parallel")),
        name="ada",
    )(cond, ada_w, ada_b.reshape(depth, 1, n))


def _norm_mod(x, g, shift, scale):
    ms = jnp.mean(x * x, axis=-1, keepdims=True)
    y = x * lax.rsqrt(ms + EPS) * g
    return y * (1.0 + scale) + shift


def _rope(a, cos, sin_lo, sin_hi):
    up = pltpu.roll(a, HEAD_W - ROPE_QUARTER, 1)
    dn = pltpu.roll(a, ROPE_QUARTER, 1)
    return a * cos + up * sin_lo + dn * sin_hi


def _rope_tables(seq, pad_rows):
    t = jnp.arange(seq)
    row = (t // GRID_W).astype(F32)[:, None]
    col = (t % GRID_W).astype(F32)[:, None]
    inv = ROPE_THETA ** (-jnp.arange(ROPE_QUARTER, dtype=F32) / ROPE_QUARTER)
    ang = jnp.concatenate([row * inv, row * inv, col * inv, col * inv], axis=1)
    cos, sin = jnp.cos(ang), jnp.sin(ang)
    first = (jnp.arange(4 * ROPE_QUARTER) % (2 * ROPE_QUARTER)) < ROPE_QUARTER
    sin_lo = jnp.where(first, -sin, 0.0)
    sin_hi = jnp.where(first, 0.0, sin)

    def finish(tab, fill):
        tab = jnp.concatenate([tab, tab], axis=1)
        return jnp.concatenate([tab, jnp.full((pad_rows, HEAD_W), fill, F32)], axis=0)

    return finish(cos, 1.0), finish(sin_lo, 0.0), finish(sin_hi, 0.0)


def _qkv_kernel(x_ref, g_ref, mod_ref, w_ref, cos_ref, slo_ref, shi_ref, o_ref, h_ref, *, rope_tiles, q_tiles):
    j = pl.program_id(1)

    @pl.when(j == 0)
    def _():
        h = _norm_mod(x_ref[...], g_ref[...], mod_ref[0, 0:1, :], mod_ref[0, 1:2, :])
        h_ref[...] = h.astype(BF16)

    acc = jnp.dot(h_ref[...], w_ref[...], preferred_element_type=F32)

    @pl.when(j < rope_tiles)
    def _():
        sc = jnp.where(j < q_tiles, DIFF_SCALE * LOG2E, 1.0).astype(F32)
        cos, slo, shi = cos_ref[...], slo_ref[...], shi_ref[...]
        for c in range(acc.shape[1] // HEAD_W):
            a = acc[:, c * HEAD_W:(c + 1) * HEAD_W]
            o_ref[:, c * HEAD_W:(c + 1) * HEAD_W] = (_rope(a, cos, slo, shi) * sc).astype(BF16)

    @pl.when(j >= rope_tiles)
    def _():
        o_ref[...] = acc.astype(BF16)


def _qkv_proj(xa, g, mod, w, tabs, *, tiles_per_batch, n_lat_tiles, n_batch):
    rows, d = xa.shape
    n = w.shape[1]
    tm, tn = ROW_TILE, QKV_COL_TILE
    row_spec = lambda shape: pl.BlockSpec(shape, lambda i, j: (i, 0))
    tab_spec = pl.BlockSpec(
        (tm, HEAD_W), lambda i, j: (jnp.where(i < n_lat_tiles, i % tiles_per_batch, tiles_per_batch), 0))
    kern = functools.partial(_qkv_kernel, rope_tiles=2 * d // tn, q_tiles=d // tn)
    return pl.pallas_call(
        kern,
        grid=(rows // tm, n // tn),
        in_specs=[
            row_spec((tm, d)),
            pl.BlockSpec((1, d), lambda i, j: (0, 0)),
            pl.BlockSpec((1, 6, d), lambda i, j: (jnp.minimum(i // tiles_per_batch, n_batch), 0, 0)),
            pl.BlockSpec((d, tn), lambda i, j: (0, j)),
            tab_spec, tab_spec, tab_spec,
        ],
        out_specs=pl.BlockSpec((tm, tn), lambda i, j: (i, j)),
        out_shape=jax.ShapeDtypeStruct((rows, n), BF16),
        scratch_shapes=[pltpu.VMEM((tm, d), BF16)],
        compiler_params=_cparams(("parallel", "arbitrary")),
        name="diff_qkv",
    )(xa, g, mod, w, *tabs)


def _mla_proj_kernel(x_ref, g_ref, mod_ref, wd_ref, qg_ref, kvg_ref, wuq_ref, wukn_ref, wuv_ref,
                     cos_ref, slo_ref, shi_ref, q_ref, k_ref, v_ref):
    h = _norm_mod(x_ref[...], g_ref[...], mod_ref[0, 0:1, :], mod_ref[0, 1:2, :]).astype(BF16)
    a = jnp.dot(h, wd_ref[...], preferred_element_type=F32)
    cq = a[:, :512]
    ckv = a[:, 512:1024]
    kpe = a[:, 1024:1152]
    cqn = cq * lax.rsqrt(jnp.sum(cq * cq, axis=-1, keepdims=True) * (1.0 / MLA_Q_LORA) + EPS) * qg_ref[...]
    ckvn = ckv * lax.rsqrt(jnp.mean(ckv * ckv, axis=-1, keepdims=True) + EPS) * kvg_ref[...]
    cqb = (cqn * (MLA_SCALE * LOG2E)).astype(BF16)
    ckvb = ckvn.astype(BF16)
    cos, slo, shi = cos_ref[...], slo_ref[...], shi_ref[...]

    kpe_r = _rope(kpe, cos, slo, shi).astype(BF16)
    group = 4
    for hg in range(HEADS // group):
        qc = jnp.dot(cqb, wuq_ref[:, hg * group * MLA_QK_W:(hg + 1) * group * MLA_QK_W],
                     preferred_element_type=F32)
        kc = jnp.dot(ckvb, wukn_ref[:, hg * group * HEAD_W:(hg + 1) * group * HEAD_W],
                     preferred_element_type=F32)
        for hh in range(group):
            o = (hg * group + hh) * MLA_QK_W
            q_ref[:, o:o + HEAD_W] = qc[:, hh * MLA_QK_W:hh * MLA_QK_W + HEAD_W].astype(BF16)
            qp = qc[:, hh * MLA_QK_W + HEAD_W:(hh + 1) * MLA_QK_W]
            q_ref[:, o + HEAD_W:o + MLA_QK_W] = _rope(qp, cos, slo, shi).astype(BF16)
            k_ref[:, o:o + HEAD_W] = kc[:, hh * HEAD_W:(hh + 1) * HEAD_W].astype(BF16)
            k_ref[:, o + HEAD_W:o + MLA_QK_W] = kpe_r
    v_ref[...] = jnp.dot(ckvb, wuv_ref[...], preferred_element_type=F32).astype(BF16)


def _mla_proj(xa, g, mod, wd, qg, kvg, wuq, wukn, wuv, tabs, *, seq, n_batch):
    rows, d = xa.shape
    tm = MLA_ROW_TILE
    tiles_per_batch = seq // tm
    n_lat_tiles = n_batch * tiles_per_batch
    full = lambda arr: pl.BlockSpec(arr.shape, lambda i: (0, 0))
    tab_spec = pl.BlockSpec(
        (tm, HEAD_W), lambda i: (jnp.where(i < n_lat_tiles, i % tiles_per_batch, tiles_per_batch), 0))
    return pl.pallas_call(
        _mla_proj_kernel,
        grid=(rows // tm,),
        in_specs=[
            pl.BlockSpec((tm, d), lambda i: (i, 0)),
            full(g),
            pl.BlockSpec((1, 6, d), lambda i: (jnp.minimum(i // tiles_per_batch, n_batch), 0, 0)),
            full(wd), full(qg), full(kvg), full(wuq), full(wukn), full(wuv),
            tab_spec, tab_spec, tab_spec,
        ],
        out_specs=[
            pl.BlockSpec((tm, HEADS * MLA_QK_W), lambda i: (i, 0)),
            pl.BlockSpec((tm, HEADS * MLA_QK_W), lambda i: (i, 0)),
            pl.BlockSpec((tm, HEADS * HEAD_W), lambda i: (i, 0)),
        ],
        out_shape=[
            jax.ShapeDtypeStruct((rows, HEADS * MLA_QK_W), BF16),
            jax.ShapeDtypeStruct((rows, HEADS * MLA_QK_W), BF16),
            jax.ShapeDtypeStruct((rows, HEADS * HEAD_W), BF16),
        ],
        compiler_params=_cparams(("parallel",)),
        name="mla_proj",
    )(xa, g, mod, wd, qg, kvg, wuq, wukn, wuv, *tabs)


def _attn_kernel(*refs, diff, has_lat, key_tile, lam_init):
    refs = list(refs)
    q_ref, kc_ref, vc_ref = refs[:3]
    pos = 3
    if has_lat:
        kl_ref, vl_ref = refs[pos:pos + 2]
        pos += 2
    if diff:
        lam_ref, sg_ref = refs[pos:pos + 2]
        pos += 2
    o_ref, m_ref, acc_ref = refs[pos:pos + 3]
    pos += 3
    if has_lat:
        s_ref, p_ref, a_ref, c_ref = refs[pos:pos + 4]

    q = q_ref[...]
    tq = q.shape[0]
    if diff:
        lane = lax.broadcasted_iota(jnp.int32, q.shape, 1)
        zero = jnp.zeros_like(q)
        q = jnp.concatenate([jnp.where(lane < DIFF_HEAD_DIM, q, zero),
                             jnp.where(lane >= DIFF_HEAD_DIM, q, zero)], axis=0)

    def scores(k):
        return lax.dot_general(k, q, (((1,), (1,)), ((), ())), preferred_element_type=F32)

    def weighted(vt, p):
        lhs = jnp.concatenate([vt, jnp.ones((SUM_ROWS, vt.shape[1]), BF16)], axis=0)
        return jnp.dot(lhs, p, preferred_element_type=F32)

    s = scores(kc_ref[...])
    m0 = jnp.max(s, axis=0, keepdims=True)
    m_ref[...] = m0
    acc_ref[...] = weighted(vc_ref[...], jnp.exp2(s - m0).astype(BF16))

    if has_lat:
        n_chunks = kl_ref.shape[0] // key_tile

        def chunk(t):
            return pl.ds(pl.multiple_of(t * key_tile, key_tile), key_tile)

        def qk(t):
            s = scores(kl_ref[chunk(t), :])
            s_ref[...] = s
            c_ref[...] = jnp.max(s, axis=0, keepdims=True)

        def sm():
            m_prev = m_ref[...]
            m_new = jnp.maximum(m_prev, c_ref[...])
            a_ref[...] = jnp.exp2(m_prev - m_new)
            p_ref[...] = jnp.exp2(s_ref[...] - m_new).astype(BF16)
            m_ref[...] = m_new

        def pv(t):
            acc_ref[...] = a_ref[...] * acc_ref[...] + weighted(vl_ref[:, chunk(t)], p_ref[...])

        qk(0)
        sm()
        qk(1)

        def body(t, carry):
            pv(t - 1)
            sm()
            qk(t + 1)
            return carry

        lax.fori_loop(1, n_chunks - 1, body, 0, unroll=2)
        pv(n_chunks - 2)
        sm()
        pv(n_chunks - 1)

    acc = acc_ref[...]
    inv_l = 1.0 / acc[HEAD_W:HEAD_W + 1, :]
    acc = acc[:HEAD_W, :]
    if diff:
        lf = lam_ref[...]
        lam = (jnp.exp(jnp.sum(lf[0:1] * lf[1:2], axis=-1, keepdims=True))
               - jnp.exp(jnp.sum(lf[2:3] * lf[3:4], axis=-1, keepdims=True)) + lam_init)
        o = acc[:, :tq] * inv_l[:, :tq] - lam * (acc[:, tq:] * inv_l[:, tq:])
        ms = jnp.mean(o * o, axis=0, keepdims=True)
        o = o * lax.rsqrt(ms + EPS) * sg_ref[...] * (1.0 - lam_init)
    else:
        o = acc * inv_l
    o_ref[...] = o.T.astype(BF16)


def _attention(q_arr, k_arr, vt_arr, *, diff, seq, n_batch, n_ctx, q_col, k_col, qk_w,
               lat_queries, lam=None, subln_g=None, lam_init=0.0, prev_out=None):
    rows = q_arr.shape[0]
    ctx_row0 = n_batch * seq
    if lat_queries:
        tq = DIFF_Q_TILE if diff else MLA_Q_TILE
        n_q = seq // tq
        q_row = lambda b, qi: b * n_q + qi
    else:
        tq = n_ctx
        n_q = 1
        q_row = lambda b, qi: ctx_row0 // n_ctx + b
    nq_cols = 2 * tq if diff else tq

    in_specs = [
        pl.BlockSpec((tq, qk_w), lambda b, h, qi: (q_row(b, qi), q_col + h)),
        pl.BlockSpec((n_ctx, qk_w), lambda b, h, qi: (ctx_row0 // n_ctx + b, k_col + h)),
        pl.BlockSpec((HEAD_W, n_ctx), lambda b, h, qi: (h, ctx_row0 // n_ctx + b)),
    ]
    args = [q_arr, k_arr, vt_arr]
    scratch = [pltpu.VMEM((1, nq_cols), F32), pltpu.VMEM((HEAD_W + SUM_ROWS, nq_cols), F32)]
    if lat_queries:
        in_specs += [
            pl.BlockSpec((seq, qk_w), lambda b, h, qi: (b, k_col + h)),
            pl.BlockSpec((HEAD_W, seq), lambda b, h, qi: (h, b)),
        ]
        args += [k_arr, vt_arr]
        scratch += [pltpu.VMEM((KEY_TILE, nq_cols), F32), pltpu.VMEM((KEY_TILE, nq_cols), BF16),
                    pltpu.VMEM((1, nq_cols), F32), pltpu.VMEM((1, nq_cols), F32)]
    if diff:
        in_specs += [
            pl.BlockSpec(lam.shape, lambda b, h, qi: (0, 0)),
            pl.BlockSpec(subln_g.shape, lambda b, h, qi: (0, 0)),
        ]
        args += [lam, subln_g]
    aliases = {}
    if prev_out is not None:
        in_specs.append(pl.BlockSpec(memory_space=pl.ANY))
        args.append(prev_out)
        aliases = {len(args) - 1: 0}

    kern = functools.partial(_attn_kernel, diff=diff, has_lat=lat_queries, key_tile=KEY_TILE, lam_init=lam_init)
    if prev_out is not None:
        inner = kern
        kern = lambda *refs: inner(*refs[:len(args) - 1], *refs[len(args):])

    return pl.pallas_call(
        kern,
        grid=(n_batch, HEADS, n_q),
        in_specs=in_specs,
        out_specs=pl.BlockSpec((tq, HEAD_W), lambda b, h, qi: (q_row(b, qi), h)),
        out_shape=jax.ShapeDtypeStruct((rows, HEADS * HEAD_W), BF16),
        scratch_shapes=scratch,
        input_output_aliases=aliases,
        compiler_params=_cparams(("parallel", "parallel", "arbitrary")),
        name=("diff" if diff else "mla") + ("_attn_lat" if lat_queries else "_attn_ctx"),
    )(*args)


def _oproj_kernel(o_ref, w_ref, x_ref, mod_ref, y_ref):
    y_ref[...] = x_ref[...] + mod_ref[0, 2:3, :] * jnp.dot(o_ref[...], w_ref[...], preferred_element_type=F32)


def _oproj(o, w, xa, mod, *, n_tiles, tiles_per_batch, n_batch):
    d = xa.shape[1]
    tm = ROW_TILE
    return pl.pallas_call(
        _oproj_kernel,
        grid=(n_tiles,),
        in_specs=[
            pl.BlockSpec((tm, o.shape[1]), lambda i: (i, 0)),
            pl.BlockSpec(w.shape, lambda i: (0, 0)),
            pl.BlockSpec((tm, d), lambda i: (i, 0)),
            pl.BlockSpec((1, 6, d), lambda i: (jnp.minimum(i // tiles_per_batch, n_batch), 0, 0)),
        ],
        out_specs=pl.BlockSpec((tm, d), lambda i: (i, 0)),
        out_shape=jax.ShapeDtypeStruct((n_tiles * tm, d), F32),
        compiler_params=_cparams(("parallel",)),
        name="oproj",
    )(o, w, xa, mod)


def _mlp_kernel(x_ref, g_ref, mod_ref, w1_ref, w2_ref, fg_ref, y_ref, h_ref, acc_ref, *, final):
    j = pl.program_id(1)

    @pl.when(j == 0)
    def _():
        h = _norm_mod(x_ref[...], g_ref[...], mod_ref[0, 3:4, :], mod_ref[0, 4:5, :])
        h_ref[...] = h.astype(BF16)
        acc_ref[...] = jnp.zeros_like(acc_ref)

    a = jnp.dot(h_ref[...], w1_ref[...], preferred_element_type=F32)
    a = jnp.maximum(a, 0.0)
    part = jnp.dot((a * a).astype(BF16), w2_ref[...], preferred_element_type=F32)

    acc_ref[...] += part

    @pl.when(j == pl.num_programs(1) - 1)
    def _():
        y = x_ref[...] + mod_ref[0, 5:6, :] * acc_ref[...]
        if final:
            ms = jnp.mean(y * y, axis=-1, keepdims=True)
            y = y * lax.rsqrt(ms + EPS) * fg_ref[...]
        y_ref[...] = y


def _mlp(xa, g, mod, w1, w2, final_g, *, n_tiles, tiles_per_batch, n_batch, final):
    d = xa.shape[1]
    ff = w1.shape[1]
    tm, tf = ROW_TILE, FF_TILE
    return pl.pallas_call(
        functools.partial(_mlp_kernel, final=final),
        grid=(n_tiles, ff // tf),
        in_specs=[
            pl.BlockSpec((tm, d), lambda i, j: (i, 0)),
            pl.BlockSpec((1, d), lambda i, j: (0, 0)),
            pl.BlockSpec((1, 6, d), lambda i, j: (jnp.minimum(i // tiles_per_batch, n_batch), 0, 0)),
            pl.BlockSpec((d, tf), lambda i, j: (0, j)),
            pl.BlockSpec((tf, d), lambda i, j: (j, 0)),
            pl.BlockSpec((1, d), lambda i, j: (0, 0)),
        ],
        out_specs=pl.BlockSpec((tm, d), lambda i, j: (i, 0)),
        out_shape=jax.ShapeDtypeStruct((n_tiles * tm, d), F32),
        scratch_shapes=[pltpu.VMEM((tm, d), BF16), pltpu.VMEM((tm, d), F32)],
        compiler_params=_cparams(("parallel", "arbitrary")),
        name="mlp",
    )(xa, g, mod, w1, w2, final_g)


def _mla_weights(wdown, q_norm_g, wuq, kv_norm_g, wukv):
    d = wdown.shape[0]
    qpad = 512 - MLA_Q_LORA
    z = lambda n: jnp.zeros((d, n), wdown.dtype)
    kv_end = MLA_Q_LORA + MLA_KV_LORA
    wd = jnp.concatenate([wdown[:, :MLA_Q_LORA], z(qpad), wdown[:, MLA_Q_LORA:kv_end],
                          wdown[:, kv_end:], z(HEAD_W - MLA_ROPE)], axis=1).astype(BF16)
    qg = jnp.pad(q_norm_g, (0, qpad)).reshape(1, -1)
    kvg = kv_norm_g.reshape(1, -1)
    wq = wuq.reshape(MLA_Q_LORA, HEADS, MLA_NOPE + MLA_ROPE)
    wq = jnp.pad(wq, ((0, qpad), (0, 0), (0, MLA_QK_W - MLA_NOPE - MLA_ROPE)))
    wq = wq.reshape(512, HEADS * MLA_QK_W).astype(BF16)
    wkv = wukv.reshape(MLA_KV_LORA, HEADS, 2 * HEAD_W)
    wukn = wkv[:, :, :HEAD_W].reshape(MLA_KV_LORA, HEADS * HEAD_W).astype(BF16)
    wuv = wkv[:, :, HEAD_W:].reshape(MLA_KV_LORA, HEADS * HEAD_W).astype(BF16)
    return wd, qg, kvg, wq, wukn, wuv


def kernel(x, c, ctx, c_ctx, ada_w, ada_b, norm_mix_g, norm_mlp_g, dif_wqkv, dif_wo, dif_lambda, dif_subln_g,
           mla_wdown, mla_q_norm_g, mla_wuq, mla_kv_norm_g, mla_wukv, mla_wo, mlp_w1, mlp_w2, final_g):
    n_batch, seq, d = x.shape
    n_ctx = ctx.shape[1]
    depth = ada_w.shape[0]
    assert d == HEADS * HEAD_W and seq % ROW_TILE == 0 and (n_batch * n_ctx) % ROW_TILE == 0
    assert seq % (2 * KEY_TILE) == 0 and n_batch + 1 <= 8 and MLA_ROW_TILE == n_ctx

    tiles_per_batch = seq // ROW_TILE
    n_lat_tiles = n_batch * tiles_per_batch
    n_all_tiles = n_lat_tiles + n_batch * n_ctx // ROW_TILE
    geom = dict(tiles_per_batch=tiles_per_batch, n_batch=n_batch)

    cond = jnp.concatenate([c, c_ctx[None, :], jnp.zeros((8 - n_batch - 1, d), F32)], axis=0)
    mods = _ada_all(cond, ada_w, ada_b)[:, :n_batch + 1].reshape(depth, n_batch + 1, 6, d)

    tabs = _rope_tables(seq, ROW_TILE)
    xa = jnp.concatenate([x.reshape(n_batch * seq, d), ctx.reshape(n_batch * n_ctx, d)], axis=0)
    row = lambda v: v.reshape(1, -1)

    for i in range(depth):
        last = i == depth - 1
        j = i // N_MIXERS
        mod = mods[i]
        att = dict(seq=seq, n_batch=n_batch, n_ctx=n_ctx)
        if i % N_MIXERS == 0:
            qkv = _qkv_proj(xa, row(norm_mix_g[i]), mod, dif_wqkv[j].astype(BF16), tabs,
                            n_lat_tiles=n_lat_tiles, **geom)
            att.update(diff=True, q_col=0, k_col=HEADS, qk_w=HEAD_W, lam=dif_lambda[j],
                       subln_g=dif_subln_g[j].reshape(-1, 1), lam_init=_diff_lambda_init(i))
            vt = qkv[:, 2 * d:].T
            o = _attention(qkv, qkv, vt, lat_queries=True, **att)
            if not last:
                o = _attention(qkv, qkv, vt, lat_queries=False, prev_out=o, **att)
            wo = dif_wo[j]
        else:
            wts = _mla_weights(mla_wdown[j], mla_q_norm_g[j], mla_wuq[j], mla_kv_norm_g[j], mla_wukv[j])
            qa, ka, va = _mla_proj(xa, row(norm_mix_g[i]), mod, *wts, tabs, seq=seq, n_batch=n_batch)
            att.update(diff=False, q_col=0, k_col=0, qk_w=MLA_QK_W)
            vt = va.T
            o = _attention(qa, ka, vt, lat_queries=True, **att)
            if not last:
                o = _attention(qa, ka, vt, lat_queries=False, prev_out=o, **att)
            wo = mla_wo[j]
        n_tiles = n_lat_tiles if last else n_all_tiles
        xa = _oproj(o, wo.astype(BF16), xa, mod, n_tiles=n_tiles, **geom)
        xa = _mlp(xa, row(norm_mlp_g[i]), mod, mlp_w1[i].astype(BF16), mlp_w2[i].astype(BF16), row(final_g),
                  n_tiles=n_tiles, final=last, **geom)

    return xa.reshape(n_batch, seq, d)
```

```python
import functools
import math

import jax
import jax.numpy as jnp
from jax import lax
from jax.experimental import pallas as pl
from jax.experimental.pallas import tpu as pltpu

F32 = jnp.float32
BF16 = jnp.bfloat16

GRID_W = 64
N_MIXERS = 2
HEADS = 16
HEAD_W = 128
DIFF_HEAD_DIM = 64
MLA_NOPE = 128
MLA_ROPE = 64
MLA_Q_LORA = 448
MLA_KV_LORA = 512
MLA_QK_W = 256
DIFF_SCALE = DIFF_HEAD_DIM ** -0.5
MLA_SCALE = (MLA_NOPE + MLA_ROPE) ** -0.5
ROPE_THETA = 10000.0
ROPE_QUARTER = 16
EPS = 1e-6
LOG2E = math.log2(math.e)
SUM_ROWS = 16

ROW_TILE = 512
MLA_ROW_TILE = 256
ADA_COL_TILE = 1024
QKV_COL_TILE = 1024
FF_TILE = 1024
DIFF_Q_TILE = 1024
MLA_Q_TILE = 2048
KEY_TILE = 512
VMEM_LIMIT = 58 * 1024 * 1024


def _cparams(sem):
    return pltpu.CompilerParams(dimension_semantics=sem, vmem_limit_bytes=VMEM_LIMIT)


def _diff_lambda_init(layer):
    return 0.8 - 0.6 * math.exp(-0.3 * layer)


def _ada_kernel(s_ref, w_ref, b_ref, o_ref):
    s = s_ref[...]
    s = s * (1.0 / (1.0 + jnp.exp(-s)))
    o_ref[0] = jnp.dot(s, w_ref[0], preferred_element_type=F32,
                       precision=lax.Precision.HIGHEST) + b_ref[0]


def _ada_all(cond, ada_w, ada_b):
    depth, d, n = ada_w.shape
    return pl.pallas_call(
        _ada_kernel,
        grid=(depth, n // ADA_COL_TILE),
        in_specs=[
            pl.BlockSpec((8, d), lambda i, j: (0, 0)),
            pl.BlockSpec((1, d, ADA_COL_TILE), lambda i, j: (i, 0, j)),
            pl.BlockSpec((1, 1, ADA_COL_TILE), lambda i, j: (i, 0, j)),
        ],
        out_specs=pl.BlockSpec((1, 8, ADA_COL_TILE), lambda i, j: (i, 0, j)),
        out_shape=jax.ShapeDtypeStruct((depth, 8, n), F32),
        compiler_params=_cparams(("parallel", "parallel")),
        name="ada",
    )(cond, ada_w, ada_b.reshape(depth, 1, n))


def _norm_mod(x, g, shift, scale):
    ms = jnp.mean(x * x, axis=-1, keepdims=True)
    y = x * lax.rsqrt(ms + EPS) * g
    return y * (1.0 + scale) + shift


def _rope(a, cos, sin_lo, sin_hi):
    up = pltpu.roll(a, HEAD_W - ROPE_QUARTER, 1)
    dn = pltpu.roll(a, ROPE_QUARTER, 1)
    return a * cos + up * sin_lo + dn * sin_hi


def _rope_tables(seq, pad_rows):
    t = jnp.arange(seq)
    row = (t // GRID_W).astype(F32)[:, None]
    col = (t % GRID_W).astype(F32)[:, None]
    inv = ROPE_THETA ** (-jnp.arange(ROPE_QUARTER, dtype=F32) / ROPE_QUARTER)
    ang = jnp.concatenate([row * inv, row * inv, col * inv, col * inv], axis=1)
    cos, sin = jnp.cos(ang), jnp.sin(ang)
    first = (jnp.arange(4 * ROPE_QUARTER) % (2 * ROPE_QUARTER)) < ROPE_QUARTER
    sin_lo = jnp.where(first, -sin, 0.0)
    sin_hi = jnp.where(first, 0.0, sin)

    def finish(tab, fill):
        tab = jnp.concatenate([tab, tab], axis=1)
        return jnp.concatenate([tab, jnp.full((pad_rows, HEAD_W), fill, F32)], axis=0)

    return finish(cos, 1.0), finish(sin_lo, 0.0), finish(sin_hi, 0.0)


def _qkv_kernel(x_ref, g_ref, mod_ref, w_ref, cos_ref, slo_ref, shi_ref, o_ref, h_ref, acc_ref, *,
                n_col, q_tiles, n_steps):
    t = pl.program_id(0)
    j = jnp.minimum(t, n_steps - 2) % n_col
    jp = jnp.maximum(t - 1, 0) % n_col

    @pl.when(t == 0)
    def _():
        acc_ref[...] = jnp.zeros_like(acc_ref)

    @pl.when(j == 0)
    def _():
        h = _norm_mod(x_ref[...], g_ref[...], mod_ref[0, 0:1, :], mod_ref[0, 1:2, :])
        h_ref[...] = h.astype(BF16)

    sc = jnp.where(jp < q_tiles, DIFF_SCALE * LOG2E, 1.0).astype(F32)
    cos, slo, shi = cos_ref[...], slo_ref[...], shi_ref[...]
    for c in range(acc_ref.shape[1] // HEAD_W):
        a = acc_ref[:, c * HEAD_W:(c + 1) * HEAD_W]
        o_ref[:, c * HEAD_W:(c + 1) * HEAD_W] = (_rope(a, cos, slo, shi) * sc).astype(BF16)
    acc_ref[...] = jnp.dot(h_ref[...], w_ref[...], preferred_element_type=F32)


def _qkv_proj(xa, g, mod, w, tabs, *, tiles_per_batch, n_lat_tiles, n_batch):
    rows, d = xa.shape
    n = w.shape[1]
    tm, tn = ROW_TILE, QKV_COL_TILE
    n_col = n // tn
    n_steps = (rows // tm) * n_col + 1
    rope_tiles = 2 * d // tn
    cur = lambda t: jnp.minimum(t, n_steps - 2)
    prev = lambda t: jnp.maximum(t - 1, 0)

    def tab_map(t):
        ip, jp = prev(t) // n_col, prev(t) % n_col
        return (jnp.where((ip < n_lat_tiles) & (jp < rope_tiles), ip % tiles_per_batch, tiles_per_batch), 0)

    tab_spec = pl.BlockSpec((tm, HEAD_W), tab_map)
    kern = functools.partial(_qkv_kernel, n_col=n_col, q_tiles=d // tn, n_steps=n_steps)
    return pl.pallas_call(
        kern,
        grid=(n_steps,),
        in_specs=[
            pl.BlockSpec((tm, d), lambda t: (cur(t) // n_col, 0)),
            pl.BlockSpec((1, d), lambda t: (0, 0)),
            pl.BlockSpec((1, 6, d), lambda t: (jnp.minimum(cur(t) // n_col // tiles_per_batch, n_batch), 0, 0)),
            pl.BlockSpec((d, tn), lambda t: (0, cur(t) % n_col)),
            tab_spec, tab_spec, tab_spec,
        ],
        out_specs=pl.BlockSpec((tm, tn), lambda t: (prev(t) // n_col, prev(t) % n_col)),
        out_shape=jax.ShapeDtypeStruct((rows, n), BF16),
        scratch_shapes=[pltpu.VMEM((tm, d), BF16), pltpu.VMEM((tm, tn), F32)],
        compiler_params=_cparams(("arbitrary",)),
        name="diff_qkv",
    )(xa, g, mod, w, *tabs)


def _mla_proj_kernel(x_ref, g_ref, mod_ref, wd_ref, qg_ref, kvg_ref, wuq_ref, wukn_ref, wuv_ref,
                     cos_ref, slo_ref, shi_ref, q_ref, k_ref, v_ref):
    h = _norm_mod(x_ref[...], g_ref[...], mod_ref[0, 0:1, :], mod_ref[0, 1:2, :]).astype(BF16)
    a = jnp.dot(h, wd_ref[...], preferred_element_type=F32)
    cq = a[:, :512]
    ckv = a[:, 512:1024]
    kpe = a[:, 1024:1152]
    cqn = cq * lax.rsqrt(jnp.sum(cq * cq, axis=-1, keepdims=True) * (1.0 / MLA_Q_LORA) + EPS) * qg_ref[...]
    ckvn = ckv * lax.rsqrt(jnp.mean(ckv * ckv, axis=-1, keepdims=True) + EPS) * kvg_ref[...]
    cqb = (cqn * (MLA_SCALE * LOG2E)).astype(BF16)
    ckvb = ckvn.astype(BF16)
    cos, slo, shi = cos_ref[...], slo_ref[...], shi_ref[...]

    kpe_r = _rope(kpe, cos, slo, shi).astype(BF16)
    group = 4
    for hg in range(HEADS // group):
        qc = jnp.dot(cqb, wuq_ref[:, hg * group * MLA_QK_W:(hg + 1) * group * MLA_QK_W],
                     preferred_element_type=F32)
        kc = jnp.dot(ckvb, wukn_ref[:, hg * group * HEAD_W:(hg + 1) * group * HEAD_W],
                     preferred_element_type=F32)
        for hh in range(group):
            o = (hg * group + hh) * MLA_QK_W
            q_ref[:, o:o + HEAD_W] = qc[:, hh * MLA_QK_W:hh * MLA_QK_W + HEAD_W].astype(BF16)
            qp = qc[:, hh * MLA_QK_W + HEAD_W:(hh + 1) * MLA_QK_W]
            q_ref[:, o + HEAD_W:o + MLA_QK_W] = _rope(qp, cos, slo, shi).astype(BF16)
            k_ref[:, o:o + HEAD_W] = kc[:, hh * HEAD_W:(hh + 1) * HEAD_W].astype(BF16)
            k_ref[:, o + HEAD_W:o + MLA_QK_W] = kpe_r
    v_ref[...] = jnp.dot(ckvb, wuv_ref[...], preferred_element_type=F32).astype(BF16)


def _mla_proj(xa, g, mod, wd, qg, kvg, wuq, wukn, wuv, tabs, *, seq, n_batch):
    rows, d = xa.shape
    tm = MLA_ROW_TILE
    tiles_per_batch = seq // tm
    n_lat_tiles = n_batch * tiles_per_batch
    full = lambda arr: pl.BlockSpec(arr.shape, lambda i: (0, 0))
    tab_spec = pl.BlockSpec(
        (tm, HEAD_W), lambda i: (jnp.where(i < n_lat_tiles, i % tiles_per_batch, tiles_per_batch), 0))
    return pl.pallas_call(
        _mla_proj_kernel,
        grid=(rows // tm,),
        in_specs=[
            pl.BlockSpec((tm, d), lambda i: (i, 0)),
            full(g),
            pl.BlockSpec((1, 6, d), lambda i: (jnp.minimum(i // tiles_per_batch, n_batch), 0, 0)),
            full(wd), full(qg), full(kvg), full(wuq), full(wukn), full(wuv),
            tab_spec, tab_spec, tab_spec,
        ],
        out_specs=[
            pl.BlockSpec((tm, HEADS * MLA_QK_W), lambda i: (i, 0)),
            pl.BlockSpec((tm, HEADS * MLA_QK_W), lambda i: (i, 0)),
            pl.BlockSpec((tm, HEADS * HEAD_W), lambda i: (i, 0)),
        ],
        out_shape=[
            jax.ShapeDtypeStruct((rows, HEADS * MLA_QK_W), BF16),
            jax.ShapeDtypeStruct((rows, HEADS * MLA_QK_W), BF16),
            jax.ShapeDtypeStruct((rows, HEADS * HEAD_W), BF16),
        ],
        compiler_params=_cparams(("parallel",)),
        name="mla_proj",
    )(xa, g, mod, wd, qg, kvg, wuq, wukn, wuv, *tabs)


def _attn_kernel(*refs, diff, has_lat, key_tile, lam_init):
    refs = list(refs)
    q_ref, kc_ref, vc_ref = refs[:3]
    pos = 3
    if has_lat:
        kl_ref, vl_ref = refs[pos:pos + 2]
        pos += 2
    if diff:
        lam_ref, sg_ref = refs[pos:pos + 2]
        pos += 2
    o_ref, m_ref, acc_ref = refs[pos:pos + 3]
    pos += 3
    if has_lat:
        s_ref, p_ref, a_ref, c_ref = refs[pos:pos + 4]

    q = q_ref[...]
    tq = q.shape[0]
    nq = m_ref.shape[1]
    if diff:
        lane = lax.broadcasted_iota(jnp.int32, q.shape, 1)
        zero = jnp.zeros_like(q)
        q = jnp.concatenate([jnp.where(lane < DIFF_HEAD_DIM, q, zero),
                             jnp.where(lane >= DIFF_HEAD_DIM, q, zero)], axis=0)

    def scores(k):
        return lax.dot_general(k, q, (((1,), (1,)), ((), ())), preferred_element_type=F32)

    def weighted(vt, p):
        lhs = jnp.concatenate([vt, jnp.ones((SUM_ROWS, vt.shape[1]), BF16)], axis=0)
        return jnp.dot(lhs, p, preferred_element_type=F32)

    s = scores(kc_ref[...])
    m0 = jnp.max(s, axis=0, keepdims=True)
    m_ref[...] = m0
    acc_ref[...] = weighted(vc_ref[...], jnp.exp2(s - m0).astype(BF16))

    if has_lat:
        n_chunks = kl_ref.shape[0] // key_tile

        def chunk(t):
            return pl.ds(pl.multiple_of(t * key_tile, key_tile), key_tile)

        def qk(t):
            s = scores(kl_ref[chunk(t), :])
            s_ref[:, :nq] = s
            c_ref[...] = jnp.max(s, axis=0, keepdims=True)

        def sm():
            m_prev = m_ref[...]
            m_new = jnp.maximum(m_prev, c_ref[...])
            a_ref[...] = jnp.exp2(m_prev - m_new)
            p_ref[:, :nq] = jnp.exp2(s_ref[:, :nq] - m_new).astype(BF16)
            m_ref[...] = m_new

        def pv(t):
            acc_ref[...] = a_ref[...] * acc_ref[...] + weighted(vl_ref[:, chunk(t)], p_ref[:, :nq])

        qk(0)
        sm()
        qk(1)

        def body(t, carry):
            pv(t - 1)
            sm()
            qk(t + 1)
            return carry

        lax.fori_loop(1, n_chunks - 1, body, 0, unroll=2)
        pv(n_chunks - 2)
        sm()
        pv(n_chunks - 1)

    acc = acc_ref[...]
    inv_l = 1.0 / acc[HEAD_W:HEAD_W + 1, :]
    acc = acc[:HEAD_W, :]
    if diff:
        lf = lam_ref[...]
        lam = (jnp.exp(jnp.sum(lf[0:1] * lf[1:2], axis=-1, keepdims=True))
               - jnp.exp(jnp.sum(lf[2:3] * lf[3:4], axis=-1, keepdims=True)) + lam_init)
        o = acc[:, :tq] * inv_l[:, :tq] - lam * (acc[:, tq:] * inv_l[:, tq:])
        ms = jnp.mean(o * o, axis=0, keepdims=True)
        o = o * lax.rsqrt(ms + EPS) * sg_ref[...] * (1.0 - lam_init)
    else:
        o = acc * inv_l
    o_ref[...] = o.T.astype(BF16)


def _attention(q_arr, k_arr, vt_arr, *, diff, seq, n_batch, n_ctx, q_col, k_col, qk_w,
               lat_queries, lam=None, subln_g=None, lam_init=0.0, prev_out=None):
    rows = q_arr.shape[0]
    ctx_row0 = n_batch * seq
    if lat_queries:
        tq = DIFF_Q_TILE if diff else MLA_Q_TILE
        n_q = seq // tq
        q_row = lambda b, qi: b * n_q + qi
    else:
        tq = n_ctx
        n_q = 1
        q_row = lambda b, qi: ctx_row0 // n_ctx + b
    nq_cols = 2 * tq if diff else tq

    in_specs = [
        pl.BlockSpec((tq, qk_w), lambda b, h, qi: (q_row(b, qi), q_col + h)),
        pl.BlockSpec((n_ctx, qk_w), lambda b, h, qi: (ctx_row0 // n_ctx + b, k_col + h)),
        pl.BlockSpec((HEAD_W, n_ctx), lambda b, h, qi: (h, ctx_row0 // n_ctx + b)),
    ]
    args = [q_arr, k_arr, vt_arr]
    scratch = [pltpu.VMEM((1, nq_cols), F32), pltpu.VMEM((HEAD_W + SUM_ROWS, nq_cols), F32)]
    if lat_queries:
        in_specs += [
            pl.BlockSpec((seq, qk_w), lambda b, h, qi: (b, k_col + h)),
            pl.BlockSpec((HEAD_W, seq), lambda b, h, qi: (h, b)),
        ]
        args += [k_arr, vt_arr]
        scratch += [pltpu.VMEM((KEY_TILE, nq_cols + HEAD_W), F32), pltpu.VMEM((KEY_TILE, nq_cols + HEAD_W), BF16),
                    pltpu.VMEM((1, nq_cols), F32), pltpu.VMEM((1, nq_cols), F32)]
    if diff:
        in_specs += [
            pl.BlockSpec(lam.shape, lambda b, h, qi: (0, 0)),
            pl.BlockSpec(subln_g.shape, lambda b, h, qi: (0, 0)),
        ]
        args += [lam, subln_g]
    aliases = {}
    if prev_out is not None:
        in_specs.append(pl.BlockSpec(memory_space=pl.ANY))
        args.append(prev_out)
        aliases = {len(args) - 1: 0}

    kern = functools.partial(_attn_kernel, diff=diff, has_lat=lat_queries, key_tile=KEY_TILE, lam_init=lam_init)
    if prev_out is not None:
        inner = kern
        kern = lambda *refs: inner(*refs[:len(args) - 1], *refs[len(args):])

    return pl.pallas_call(
        kern,
        grid=(n_batch, HEADS, n_q),
        in_specs=in_specs,
        out_specs=pl.BlockSpec((tq, HEAD_W), lambda b, h, qi: (q_row(b, qi), h)),
        out_shape=jax.ShapeDtypeStruct((rows, HEADS * HEAD_W), BF16),
        scratch_shapes=scratch,
        input_output_aliases=aliases,
        compiler_params=_cparams(("parallel", "parallel", "arbitrary")),
        name=("diff" if diff else "mla") + ("_attn_lat" if lat_queries else "_attn_ctx"),
    )(*args)


def _tail_kernel(o_ref, wo_ref, x_ref, g_ref, mod_ref, w1_ref, w2_ref, fg_ref, y_ref, h_ref, acc_ref, *, final):
    j = pl.program_id(1)

    @pl.when(j == 0)
    def _():
        x1 = x_ref[...] + mod_ref[0, 2:3, :] * jnp.dot(o_ref[...], wo_ref[...], preferred_element_type=F32)
        y_ref[...] = x1
        h = _norm_mod(x1, g_ref[...], mod_ref[0, 3:4, :], mod_ref[0, 4:5, :])
        h_ref[...] = h.astype(BF16)
        acc_ref[...] = jnp.zeros_like(acc_ref)

    a = jnp.dot(h_ref[...], w1_ref[...], preferred_element_type=F32)
    a = jnp.maximum(a, 0.0)
    acc_ref[...] += jnp.dot((a * a).astype(BF16), w2_ref[...], preferred_element_type=F32)

    @pl.when(j == pl.num_programs(1) - 1)
    def _():
        y = y_ref[...] + mod_ref[0, 5:6, :] * acc_ref[...]
        if final:
            ms = jnp.mean(y * y, axis=-1, keepdims=True)
            y = y * lax.rsqrt(ms + EPS) * fg_ref[...]
        y_ref[...] = y


def _tail(o, wo, xa, g, mod, w1, w2, final_g, *, n_tiles, tiles_per_batch, n_batch, final):
    d = xa.shape[1]
    ff = w1.shape[1]
    tm, tf = ROW_TILE, FF_TILE
    once = pl.Buffered(1)
    return pl.pallas_call(
        functools.partial(_tail_kernel, final=final),
        grid=(n_tiles, ff // tf),
        in_specs=[
            pl.BlockSpec((tm, o.shape[1]), lambda i, j: (i, 0)),
            pl.BlockSpec(wo.shape, lambda i, j: (0, 0), pipeline_mode=once),
            pl.BlockSpec((tm, d), lambda i, j: (i, 0)),
            pl.BlockSpec((1, d), lambda i, j: (0, 0)),
            pl.BlockSpec((1, 6, d), lambda i, j: (jnp.minimum(i // tiles_per_batch, n_batch), 0, 0)),
            pl.BlockSpec((d, tf), lambda i, j: (0, j)),
            pl.BlockSpec((tf, d), lambda i, j: (j, 0)),
            pl.BlockSpec((1, d), lambda i, j: (0, 0)),
        ],
        out_specs=pl.BlockSpec((tm, d), lambda i, j: (i, 0)),
        out_shape=jax.ShapeDtypeStruct((n_tiles * tm, d), F32),
        scratch_shapes=[pltpu.VMEM((tm, d), BF16), pltpu.VMEM((tm, d), F32)],
        compiler_params=_cparams(("parallel", "arbitrary")),
        name="tail",
    )(o, wo, xa, g, mod, w1, w2, final_g)


def _mla_weights(wdown, q_norm_g, wuq, kv_norm_g, wukv):
    d = wdown.shape[0]
    qpad = 512 - MLA_Q_LORA
    z = lambda n: jnp.zeros((d, n), wdown.dtype)
    kv_end = MLA_Q_LORA + MLA_KV_LORA
    wd = jnp.concatenate([wdown[:, :MLA_Q_LORA], z(qpad), wdown[:, MLA_Q_LORA:kv_end],
                          wdown[:, kv_end:], z(HEAD_W - MLA_ROPE)], axis=1).astype(BF16)
    qg = jnp.pad(q_norm_g, (0, qpad)).reshape(1, -1)
    kvg = kv_norm_g.reshape(1, -1)
    wq = wuq.reshape(MLA_Q_LORA, HEADS, MLA_NOPE + MLA_ROPE)
    wq = jnp.pad(wq, ((0, qpad), (0, 0), (0, MLA_QK_W - MLA_NOPE - MLA_ROPE)))
    wq = wq.reshape(512, HEADS * MLA_QK_W).astype(BF16)
    wkv = wukv.reshape(MLA_KV_LORA, HEADS, 2 * HEAD_W)
    wukn = wkv[:, :, :HEAD_W].reshape(MLA_KV_LORA, HEADS * HEAD_W).astype(BF16)
    wuv = wkv[:, :, HEAD_W:].reshape(MLA_KV_LORA, HEADS * HEAD_W).astype(BF16)
    return wd, qg, kvg, wq, wukn, wuv


def kernel(x, c, ctx, c_ctx, ada_w, ada_b, norm_mix_g, norm_mlp_g, dif_wqkv, dif_wo, dif_lambda, dif_subln_g,
           mla_wdown, mla_q_norm_g, mla_wuq, mla_kv_norm_g, mla_wukv, mla_wo, mlp_w1, mlp_w2, final_g):
    n_batch, seq, d = x.shape
    n_ctx = ctx.shape[1]
    depth = ada_w.shape[0]
    assert d == HEADS * HEAD_W and seq % ROW_TILE == 0 and (n_batch * n_ctx) % ROW_TILE == 0
    assert seq % (2 * KEY_TILE) == 0 and n_batch + 1 <= 8 and MLA_ROW_TILE == n_ctx

    tiles_per_batch = seq // ROW_TILE
    n_lat_tiles = n_batch * tiles_per_batch
    n_all_tiles = n_lat_tiles + n_batch * n_ctx // ROW_TILE
    geom = dict(tiles_per_batch=tiles_per_batch, n_batch=n_batch)

    cond = jnp.concatenate([c, c_ctx[None, :], jnp.zeros((8 - n_batch - 1, d), F32)], axis=0)
    mods = _ada_all(cond, ada_w, ada_b)[:, :n_batch + 1].reshape(depth, n_batch + 1, 6, d)

    tabs = _rope_tables(seq, ROW_TILE)
    xa = jnp.concatenate([x.reshape(n_batch * seq, d), ctx.reshape(n_batch * n_ctx, d)], axis=0)
    row = lambda v: v.reshape(1, -1)

    for i in range(depth):
        last = i == depth - 1
        j = i // N_MIXERS
        mod = mods[i]
        att = dict(seq=seq, n_batch=n_batch, n_ctx=n_ctx)
        if i % N_MIXERS == 0:
            qkv = _qkv_proj(xa, row(norm_mix_g[i]), mod, dif_wqkv[j].astype(BF16), tabs,
                            n_lat_tiles=n_lat_tiles, **geom)
            att.update(diff=True, q_col=0, k_col=HEADS, qk_w=HEAD_W, lam=dif_lambda[j],
                       subln_g=dif_subln_g[j].reshape(-1, 1), lam_init=_diff_lambda_init(i))
            vt = qkv[:, 2 * d:].T
            o = _attention(qkv, qkv, vt, lat_queries=True, **att)
            if not last:
                o = _attention(qkv, qkv, vt, lat_queries=False, prev_out=o, **att)
            wo = dif_wo[j]
        else:
            wts = _mla_weights(mla_wdown[j], mla_q_norm_g[j], mla_wuq[j], mla_kv_norm_g[j], mla_wukv[j])
            qa, ka, va = _mla_proj(xa, row(norm_mix_g[i]), mod, *wts, tabs, seq=seq, n_batch=n_batch)
            att.update(diff=False, q_col=0, k_col=0, qk_w=MLA_QK_W)
            vt = va.T
            o = _attention(qa, ka, vt, lat_queries=True, **att)
            if not last:
                o = _attention(qa, ka, vt, lat_queries=False, prev_out=o, **att)
            wo = mla_wo[j]
        n_tiles = n_lat_tiles if last else n_all_tiles
        xa = _tail(o, wo.astype(BF16), xa, row(norm_mlp_g[i]), mod, mlp_w1[i].astype(BF16), mlp_w2[i].astype(BF16),
                   row(final_g), n_tiles=n_tiles, final=last, **geom)

    return xa.reshape(n_batch, seq, d)
```

```python
import functools
import math

import jax
import jax.numpy as jnp
from jax import lax
from jax.experimental import pallas as pl
from jax.experimental.pallas import tpu as pltpu

F32 = jnp.float32
BF16 = jnp.bfloat16

GRID_W = 64
N_MIXERS = 2
HEADS = 16
HEAD_W = 128
DIFF_HEAD_DIM = 64
MLA_NOPE = 128
MLA_ROPE = 64
MLA_Q_LORA = 448
MLA_KV_LORA = 512
MLA_QK_W = 256
DIFF_SCALE = DIFF_HEAD_DIM ** -0.5
MLA_SCALE = (MLA_NOPE + MLA_ROPE) ** -0.5
ROPE_THETA = 10000.0
ROPE_QUARTER = 16
EPS = 1e-6
LOG2E = math.log2(math.e)
SUM_ROWS = 16

ROW_TILE = 512
MLA_ROW_TILE = 256
ADA_COL_TILE = 1024
QKV_COL_TILE = 1024
FF_TILE = 1024
DIFF_Q_TILE = 1024
MLA_Q_TILE = 2048
KEY_TILE = 512
VMEM_LIMIT = 58 * 1024 * 1024


def _cparams(sem):
    return pltpu.CompilerParams(dimension_semantics=sem, vmem_limit_bytes=VMEM_LIMIT)


def _diff_lambda_init(layer):
    return 0.8 - 0.6 * math.exp(-0.3 * layer)


def _ada_kernel(s_ref, w_ref, b_ref, o_ref):
    s = s_ref[...]
    s = s * (1.0 / (1.0 + jnp.exp(-s)))
    o_ref[0] = jnp.dot(s, w_ref[0], preferred_element_type=F32,
                       precision=lax.Precision.HIGHEST) + b_ref[0]


def _ada_all(cond, ada_w, ada_b):
    depth, d, n = ada_w.shape
    return pl.pallas_call(
        _ada_kernel,
        grid=(depth, n // ADA_COL_TILE),
        in_specs=[
            pl.BlockSpec((8, d), lambda i, j: (0, 0)),
            pl.BlockSpec((1, d, ADA_COL_TILE), lambda i, j: (i, 0, j)),
            pl.BlockSpec((1, 1, ADA_COL_TILE), lambda i, j: (i, 0, j)),
        ],
        out_specs=pl.BlockSpec((1, 8, ADA_COL_TILE), lambda i, j: (i, 0, j)),
        out_shape=jax.ShapeDtypeStruct((depth, 8, n), F32),
        compiler_params=_cparams(("parallel", "parallel")),
        name="ada",
    )(cond, ada_w, ada_b.reshape(depth, 1, n))


def _norm_mod(x, g, shift, scale):
    ms = jnp.mean(x * x, axis=-1, keepdims=True)
    y = x * lax.rsqrt(ms + EPS) * g
    return y * (1.0 + scale) + shift


def _rope(a, cos, sin_lo, sin_hi):
    up = pltpu.roll(a, HEAD_W - ROPE_QUARTER, 1)
    dn = pltpu.roll(a, ROPE_QUARTER, 1)
    return a * cos + up * sin_lo + dn * sin_hi


def _rope_tables(seq, pad_rows):
    t = jnp.arange(seq)
    row = (t // GRID_W).astype(F32)[:, None]
    col = (t % GRID_W).astype(F32)[:, None]
    inv = ROPE_THETA ** (-jnp.arange(ROPE_QUARTER, dtype=F32) / ROPE_QUARTER)
    ang = jnp.concatenate([row * inv, row * inv, col * inv, col * inv], axis=1)
    cos, sin = jnp.cos(ang), jnp.sin(ang)
    first = (jnp.arange(4 * ROPE_QUARTER) % (2 * ROPE_QUARTER)) < ROPE_QUARTER
    sin_lo = jnp.where(first, -sin, 0.0)
    sin_hi = jnp.where(first, 0.0, sin)

    def finish(tab, fill):
        tab = jnp.concatenate([tab, tab], axis=1)
        return jnp.concatenate([tab, jnp.full((pad_rows, HEAD_W), fill, F32)], axis=0)

    return finish(cos, 1.0), finish(sin_lo, 0.0), finish(sin_hi, 0.0)


def _qkv_kernel(x_ref, g_ref, mod_ref, w_ref, cos_ref, slo_ref, shi_ref, o_ref, h_ref, acc_ref, *,
                n_col, q_tiles, n_steps):
    t = pl.program_id(0)
    j = jnp.minimum(t, n_steps - 2) % n_col
    jp = jnp.maximum(t - 1, 0) % n_col

    @pl.when(t == 0)
    def _():
        acc_ref[...] = jnp.zeros_like(acc_ref)

    @pl.when(j == 0)
    def _():
        h = _norm_mod(x_ref[...], g_ref[...], mod_ref[0, 0:1, :], mod_ref[0, 1:2, :])
        h_ref[...] = h.astype(BF16)

    sc = jnp.where(jp < q_tiles, DIFF_SCALE * LOG2E, 1.0).astype(F32)
    cos, slo, shi = cos_ref[...], slo_ref[...], shi_ref[...]
    for c in range(acc_ref.shape[1] // HEAD_W):
        a = acc_ref[:, c * HEAD_W:(c + 1) * HEAD_W]
        o_ref[:, c * HEAD_W:(c + 1) * HEAD_W] = (_rope(a, cos, slo, shi) * sc).astype(BF16)
    acc_ref[...] = jnp.dot(h_ref[...], w_ref[...], preferred_element_type=F32)


def _qkv_proj(xa, g, mod, w, tabs, *, tiles_per_batch, n_lat_tiles, n_batch):
    rows, d = xa.shape
    n = w.shape[1]
    tm, tn = ROW_TILE, QKV_COL_TILE
    n_col = n // tn
    n_steps = (rows // tm) * n_col + 1
    rope_tiles = 2 * d // tn
    cur = lambda t: jnp.minimum(t, n_steps - 2)
    prev = lambda t: jnp.maximum(t - 1, 0)

    def tab_map(t):
        ip, jp = prev(t) // n_col, prev(t) % n_col
        return (jnp.where((ip < n_lat_tiles) & (jp < rope_tiles), ip % tiles_per_batch, tiles_per_batch), 0)

    tab_spec = pl.BlockSpec((tm, HEAD_W), tab_map)
    kern = functools.partial(_qkv_kernel, n_col=n_col, q_tiles=d // tn, n_steps=n_steps)
    return pl.pallas_call(
        kern,
        grid=(n_steps,),
        in_specs=[
            pl.BlockSpec((tm, d), lambda t: (cur(t) // n_col, 0)),
            pl.BlockSpec((1, d), lambda t: (0, 0)),
            pl.BlockSpec((1, 6, d), lambda t: (jnp.minimum(cur(t) // n_col // tiles_per_batch, n_batch), 0, 0)),
            pl.BlockSpec((d, tn), lambda t: (0, cur(t) % n_col)),
            tab_spec, tab_spec, tab_spec,
        ],
        out_specs=pl.BlockSpec((tm, tn), lambda t: (prev(t) // n_col, prev(t) % n_col)),
        out_shape=jax.ShapeDtypeStruct((rows, n), BF16),
        scratch_shapes=[pltpu.VMEM((tm, d), BF16), pltpu.VMEM((tm, tn), F32)],
        compiler_params=_cparams(("arbitrary",)),
        name="diff_qkv",
    )(xa, g, mod, w, *tabs)


def _mla_proj_kernel(x_ref, g_ref, mod_ref, wd_ref, qg_ref, kvg_ref, wuq_ref, wukn_ref, wuv_ref,
                     cos_ref, slo_ref, shi_ref, q_ref, k_ref, v_ref):
    h = _norm_mod(x_ref[...], g_ref[...], mod_ref[0, 0:1, :], mod_ref[0, 1:2, :]).astype(BF16)
    a = jnp.dot(h, wd_ref[...], preferred_element_type=F32)
    cq = a[:, :512]
    ckv = a[:, 512:1024]
    kpe = a[:, 1024:1152]
    cqn = cq * lax.rsqrt(jnp.sum(cq * cq, axis=-1, keepdims=True) * (1.0 / MLA_Q_LORA) + EPS) * qg_ref[...]
    ckvn = ckv * lax.rsqrt(jnp.mean(ckv * ckv, axis=-1, keepdims=True) + EPS) * kvg_ref[...]
    cqb = (cqn * (MLA_SCALE * LOG2E)).astype(BF16)
    ckvb = ckvn.astype(BF16)
    cos, slo, shi = cos_ref[...], slo_ref[...], shi_ref[...]

    kpe_r = _rope(kpe, cos, slo, shi).astype(BF16)
    group = 4
    for hg in range(HEADS // group):
        qc = jnp.dot(cqb, wuq_ref[:, hg * group * MLA_QK_W:(hg + 1) * group * MLA_QK_W],
                     preferred_element_type=F32)
        kc = jnp.dot(ckvb, wukn_ref[:, hg * group * HEAD_W:(hg + 1) * group * HEAD_W],
                     preferred_element_type=F32)
        for hh in range(group):
            o = (hg * group + hh) * MLA_QK_W
            q_ref[:, o:o + HEAD_W] = qc[:, hh * MLA_QK_W:hh * MLA_QK_W + HEAD_W].astype(BF16)
            qp = qc[:, hh * MLA_QK_W + HEAD_W:(hh + 1) * MLA_QK_W]
            q_ref[:, o + HEAD_W:o + MLA_QK_W] = _rope(qp, cos, slo, shi).astype(BF16)
            k_ref[:, o:o + HEAD_W] = kc[:, hh * HEAD_W:(hh + 1) * HEAD_W].astype(BF16)
            k_ref[:, o + HEAD_W:o + MLA_QK_W] = kpe_r
    v_ref[...] = jnp.dot(ckvb, wuv_ref[...], preferred_element_type=F32).astype(BF16)


def _mla_proj(xa, g, mod, wd, qg, kvg, wuq, wukn, wuv, tabs, *, seq, n_batch):
    rows, d = xa.shape
    tm = MLA_ROW_TILE
    tiles_per_batch = seq // tm
    n_lat_tiles = n_batch * tiles_per_batch
    full = lambda arr: pl.BlockSpec(arr.shape, lambda i: (0, 0))
    tab_spec = pl.BlockSpec(
        (tm, HEAD_W), lambda i: (jnp.where(i < n_lat_tiles, i % tiles_per_batch, tiles_per_batch), 0))
    return pl.pallas_call(
        _mla_proj_kernel,
        grid=(rows // tm,),
        in_specs=[
            pl.BlockSpec((tm, d), lambda i: (i, 0)),
            full(g),
            pl.BlockSpec((1, 6, d), lambda i: (jnp.minimum(i // tiles_per_batch, n_batch), 0, 0)),
            full(wd), full(qg), full(kvg), full(wuq), full(wukn), full(wuv),
            tab_spec, tab_spec, tab_spec,
        ],
        out_specs=[
            pl.BlockSpec((tm, HEADS * MLA_QK_W), lambda i: (i, 0)),
            pl.BlockSpec((tm, HEADS * MLA_QK_W), lambda i: (i, 0)),
            pl.BlockSpec((tm, HEADS * HEAD_W), lambda i: (i, 0)),
        ],
        out_shape=[
            jax.ShapeDtypeStruct((rows, HEADS * MLA_QK_W), BF16),
            jax.ShapeDtypeStruct((rows, HEADS * MLA_QK_W), BF16),
            jax.ShapeDtypeStruct((rows, HEADS * HEAD_W), BF16),
        ],
        compiler_params=_cparams(("parallel",)),
        name="mla_proj",
    )(xa, g, mod, wd, qg, kvg, wuq, wukn, wuv, *tabs)


def _attn_kernel(*refs, diff, has_lat, key_tile, lam_init):
    refs = list(refs)
    q_ref, kc_ref, vc_ref = refs[:3]
    pos = 3
    if has_lat:
        kl_ref, vl_ref = refs[pos:pos + 2]
        pos += 2
    if diff:
        lam_ref, sg_ref = refs[pos:pos + 2]
        pos += 2
    o_ref, m_ref, acc_ref = refs[pos:pos + 3]
    pos += 3
    if has_lat:
        s_ref, p_ref, a_ref, c_ref = refs[pos:pos + 4]

    q = q_ref[...]
    tq = q.shape[0]
    nq = m_ref.shape[1]
    if diff:
        lane = lax.broadcasted_iota(jnp.int32, q.shape, 1)
        zero = jnp.zeros_like(q)
        q = jnp.concatenate([jnp.where(lane < DIFF_HEAD_DIM, q, zero),
                             jnp.where(lane >= DIFF_HEAD_DIM, q, zero)], axis=0)

    def scores(k):
        return lax.dot_general(k, q, (((1,), (1,)), ((), ())), preferred_element_type=F32)

    def weighted(vt, p):
        lhs = jnp.concatenate([vt, jnp.ones((SUM_ROWS, vt.shape[1]), BF16)], axis=0)
        return jnp.dot(lhs, p, preferred_element_type=F32)

    s = scores(kc_ref[...])
    m0 = jnp.max(s, axis=0, keepdims=True)
    m_ref[...] = m0
    acc_ref[...] = weighted(vc_ref[...], jnp.exp2(s - m0).astype(BF16))

    if has_lat:
        n_chunks = kl_ref.shape[0] // key_tile

        def chunk(t):
            return pl.ds(pl.multiple_of(t * key_tile, key_tile), key_tile)

        def qk(t):
            s = scores(kl_ref[chunk(t), :])
            s_ref[:, :nq] = s
            c_ref[...] = jnp.max(s, axis=0, keepdims=True)

        def sm():
            m_prev = m_ref[...]
            m_new = jnp.maximum(m_prev, c_ref[...])
            a_ref[...] = jnp.exp2(m_prev - m_new)
            p_ref[:, :nq] = jnp.exp2(s_ref[:, :nq] - m_new).astype(BF16)
            m_ref[...] = m_new

        def pv(t):
            acc_ref[...] = a_ref[...] * acc_ref[...] + weighted(vl_ref[:, chunk(t)], p_ref[:, :nq])

        qk(0)
        sm()
        qk(1)

        def body(t, carry):
            pv(t - 1)
            sm()
            qk(t + 1)
            return carry

        lax.fori_loop(1, n_chunks - 1, body, 0, unroll=2)
        pv(n_chunks - 2)
        sm()
        pv(n_chunks - 1)

    acc = acc_ref[...]
    inv_l = 1.0 / acc[HEAD_W:HEAD_W + 1, :]
    acc = acc[:HEAD_W, :]
    if diff:
        lf = lam_ref[...]
        lam = (jnp.exp(jnp.sum(lf[0:1] * lf[1:2], axis=-1, keepdims=True))
               - jnp.exp(jnp.sum(lf[2:3] * lf[3:4], axis=-1, keepdims=True)) + lam_init)
        o = acc[:, :tq] * inv_l[:, :tq] - lam * (acc[:, tq:] * inv_l[:, tq:])
        ms = jnp.mean(o * o, axis=0, keepdims=True)
        o = o * lax.rsqrt(ms + EPS) * sg_ref[...] * (1.0 - lam_init)
    else:
        o = acc * inv_l
    o_ref[...] = o.T.astype(BF16)


def _attention(q_arr, k_arr, vt_arr, *, diff, seq, n_batch, n_ctx, q_col, k_col, qk_w,
               lat_queries, lam=None, subln_g=None, lam_init=0.0, prev_out=None):
    rows = q_arr.shape[0]
    ctx_row0 = n_batch * seq
    if lat_queries:
        tq = DIFF_Q_TILE if diff else MLA_Q_TILE
        n_q = seq // tq
        q_row = lambda b, qi: b * n_q + qi
    else:
        tq = n_ctx
        n_q = 1
        q_row = lambda b, qi: ctx_row0 // n_ctx + b
    nq_cols = 2 * tq if diff else tq

    in_specs = [
        pl.BlockSpec((tq, qk_w), lambda b, h, qi: (q_row(b, qi), q_col + h)),
        pl.BlockSpec((n_ctx, qk_w), lambda b, h, qi: (ctx_row0 // n_ctx + b, k_col + h)),
        pl.BlockSpec((HEAD_W, n_ctx), lambda b, h, qi: (h, ctx_row0 // n_ctx + b)),
    ]
    args = [q_arr, k_arr, vt_arr]
    scratch = [pltpu.VMEM((1, nq_cols), F32), pltpu.VMEM((HEAD_W + SUM_ROWS, nq_cols), F32)]
    if lat_queries:
        in_specs += [
            pl.BlockSpec((seq, qk_w), lambda b, h, qi: (b, k_col + h)),
            pl.BlockSpec((HEAD_W, seq), lambda b, h, qi: (h, b)),
        ]
        args += [k_arr, vt_arr]
        scratch += [pltpu.VMEM((KEY_TILE, nq_cols), F32), pltpu.VMEM((KEY_TILE, nq_cols), BF16),
                    pltpu.VMEM((1, nq_cols), F32), pltpu.VMEM((1, nq_cols), F32)]
    if diff:
        in_specs += [
            pl.BlockSpec(lam.shape, lambda b, h, qi: (0, 0)),
            pl.BlockSpec(subln_g.shape, lambda b, h, qi: (0, 0)),
        ]
        args += [lam, subln_g]
    aliases = {}
    if prev_out is not None:
        in_specs.append(pl.BlockSpec(memory_space=pl.ANY))
        args.append(prev_out)
        aliases = {len(args) - 1: 0}

    kern = functools.partial(_attn_kernel, diff=diff, has_lat=lat_queries, key_tile=KEY_TILE, lam_init=lam_init)
    if prev_out is not None:
        inner = kern
        kern = lambda *refs: inner(*refs[:len(args) - 1], *refs[len(args):])

    return pl.pallas_call(
        kern,
        grid=(n_batch, HEADS, n_q),
        in_specs=in_specs,
        out_specs=pl.BlockSpec((tq, HEAD_W), lambda b, h, qi: (q_row(b, qi), h)),
        out_shape=jax.ShapeDtypeStruct((rows, HEADS * HEAD_W), BF16),
        scratch_shapes=scratch,
        input_output_aliases=aliases,
        compiler_params=_cparams(("parallel", "parallel", "arbitrary")),
        name=("diff" if diff else "mla") + ("_attn_lat" if lat_queries else "_attn_ctx"),
    )(*args)


def _tail_kernel(o_ref, wo_ref, x_ref, g_ref, mod_ref, w1_ref, w2_ref, fg_ref, y_ref, h_ref, acc_ref, *, final):
    j = pl.program_id(1)

    @pl.when(j == 0)
    def _():
        x1 = x_ref[...] + mod_ref[0, 2:3, :] * jnp.dot(o_ref[...], wo_ref[...], preferred_element_type=F32)
        y_ref[...] = x1
        h = _norm_mod(x1, g_ref[...], mod_ref[0, 3:4, :], mod_ref[0, 4:5, :])
        h_ref[...] = h.astype(BF16)
        acc_ref[...] = jnp.zeros_like(acc_ref)

    a = jnp.dot(h_ref[...], w1_ref[...], preferred_element_type=F32)
    a = jnp.maximum(a, 0.0)
    acc_ref[...] += jnp.dot((a * a).astype(BF16), w2_ref[...], preferred_element_type=F32)

    @pl.when(j == pl.num_programs(1) - 1)
    def _():
        y = y_ref[...] + mod_ref[0, 5:6, :] * acc_ref[...]
        if final:
            ms = jnp.mean(y * y, axis=-1, keepdims=True)
            y = y * lax.rsqrt(ms + EPS) * fg_ref[...]
        y_ref[...] = y


def _tail(o, wo, xa, g, mod, w1, w2, final_g, *, n_tiles, tiles_per_batch, n_batch, final):
    d = xa.shape[1]
    ff = w1.shape[1]
    tm, tf = ROW_TILE, FF_TILE
    once = pl.Buffered(1)
    return pl.pallas_call(
        functools.partial(_tail_kernel, final=final),
        grid=(n_tiles, ff // tf),
        in_specs=[
            pl.BlockSpec((tm, o.shape[1]), lambda i, j: (i, 0)),
            pl.BlockSpec(wo.shape, lambda i, j: (0, 0), pipeline_mode=once),
            pl.BlockSpec((tm, d), lambda i, j: (i, 0)),
            pl.BlockSpec((1, d), lambda i, j: (0, 0)),
            pl.BlockSpec((1, 6, d), lambda i, j: (jnp.minimum(i // tiles_per_batch, n_batch), 0, 0)),
            pl.BlockSpec((d, tf), lambda i, j: (0, j)),
            pl.BlockSpec((tf, d), lambda i, j: (j, 0)),
            pl.BlockSpec((1, d), lambda i, j: (0, 0)),
        ],
        out_specs=pl.BlockSpec((tm, d), lambda i, j: (i, 0)),
        out_shape=jax.ShapeDtypeStruct((n_tiles * tm, d), F32),
        scratch_shapes=[pltpu.VMEM((tm, d), BF16), pltpu.VMEM((tm, d), F32)],
        compiler_params=_cparams(("parallel", "arbitrary")),
        name="tail",
    )(o, wo, xa, g, mod, w1, w2, final_g)


def _mla_weights(wdown, q_norm_g, wuq, kv_norm_g, wukv):
    d = wdown.shape[0]
    qpad = 512 - MLA_Q_LORA
    z = lambda n: jnp.zeros((d, n), wdown.dtype)
    kv_end = MLA_Q_LORA + MLA_KV_LORA
    wd = jnp.concatenate([wdown[:, :MLA_Q_LORA], z(qpad), wdown[:, MLA_Q_LORA:kv_end],
                          wdown[:, kv_end:], z(HEAD_W - MLA_ROPE)], axis=1).astype(BF16)
    qg = jnp.pad(q_norm_g, (0, qpad)).reshape(1, -1)
    kvg = kv_norm_g.reshape(1, -1)
    wq = wuq.reshape(MLA_Q_LORA, HEADS, MLA_NOPE + MLA_ROPE)
    wq = jnp.pad(wq, ((0, qpad), (0, 0), (0, MLA_QK_W - MLA_NOPE - MLA_ROPE)))
    wq = wq.reshape(512, HEADS * MLA_QK_W).astype(BF16)
    wkv = wukv.reshape(MLA_KV_LORA, HEADS, 2 * HEAD_W)
    wukn = wkv[:, :, :HEAD_W].reshape(MLA_KV_LORA, HEADS * HEAD_W).astype(BF16)
    wuv = wkv[:, :, HEAD_W:].reshape(MLA_KV_LORA, HEADS * HEAD_W).astype(BF16)
    return wd, qg, kvg, wq, wukn, wuv


def kernel(x, c, ctx, c_ctx, ada_w, ada_b, norm_mix_g, norm_mlp_g, dif_wqkv, dif_wo, dif_lambda, dif_subln_g,
           mla_wdown, mla_q_norm_g, mla_wuq, mla_kv_norm_g, mla_wukv, mla_wo, mlp_w1, mlp_w2, final_g):
    n_batch, seq, d = x.shape
    n_ctx = ctx.shape[1]
    depth = ada_w.shape[0]
    assert d == HEADS * HEAD_W and seq % ROW_TILE == 0 and (n_batch * n_ctx) % ROW_TILE == 0
    assert seq % (2 * KEY_TILE) == 0 and n_batch + 1 <= 8 and MLA_ROW_TILE == n_ctx

    tiles_per_batch = seq // ROW_TILE
    n_lat_tiles = n_batch * tiles_per_batch
    n_all_tiles = n_lat_tiles + n_batch * n_ctx // ROW_TILE
    geom = dict(tiles_per_batch=tiles_per_batch, n_batch=n_batch)

    cond = jnp.concatenate([c, c_ctx[None, :], jnp.zeros((8 - n_batch - 1, d), F32)], axis=0)
    mods = _ada_all(cond, ada_w, ada_b)[:, :n_batch + 1].reshape(depth, n_batch + 1, 6, d)

    tabs = _rope_tables(seq, ROW_TILE)
    xa = jnp.concatenate([x.reshape(n_batch * seq, d), ctx.reshape(n_batch * n_ctx, d)], axis=0)
    row = lambda v: v.reshape(1, -1)

    for i in range(depth):
        last = i == depth - 1
        j = i // N_MIXERS
        mod = mods[i]
        att = dict(seq=seq, n_batch=n_batch, n_ctx=n_ctx)
        if i % N_MIXERS == 0:
            qkv = _qkv_proj(xa, row(norm_mix_g[i]), mod, dif_wqkv[j].astype(BF16), tabs,
                            n_lat_tiles=n_lat_tiles, **geom)
            att.update(diff=True, q_col=0, k_col=HEADS, qk_w=HEAD_W, lam=dif_lambda[j],
                       subln_g=dif_subln_g[j].reshape(-1, 1), lam_init=_diff_lambda_init(i))
            vt = qkv[:, 2 * d:].T
            o = _attention(qkv, qkv, vt, lat_queries=True, **att)
            if not last:
                o = _attention(qkv, qkv, vt, lat_queries=False, prev_out=o, **att)
            wo = dif_wo[j]
        else:
            wts = _mla_weights(mla_wdown[j], mla_q_norm_g[j], mla_wuq[j], mla_kv_norm_g[j], mla_wukv[j])
            qa, ka, va = _mla_proj(xa, row(norm_mix_g[i]), mod, *wts, tabs, seq=seq, n_batch=n_batch)
            att.update(diff=False, q_col=0, k_col=0, qk_w=MLA_QK_W)
            vt = va.T
            o = _attention(qa, ka, vt, lat_queries=True, **att)
            if not last:
                o = _attention(qa, ka, vt, lat_queries=False, prev_out=o, **att)
            wo = mla_wo[j]
        n_tiles = n_lat_tiles if last else n_all_tiles
        xa = _tail(o, wo.astype(BF16), xa, row(norm_mlp_g[i]), mod, mlp_w1[i].astype(BF16), mlp_w2[i].astype(BF16),
                   row(final_g), n_tiles=n_tiles, final=last, **geom)

    return xa.reshape(n_batch, seq, d)
```

```python
import functools
import math

import jax
import jax.numpy as jnp
from jax import lax
from jax.experimental import pallas as pl
from jax.experimental.pallas import tpu as pltpu

F32 = jnp.float32
BF16 = jnp.bfloat16

GRID_W = 64
N_MIXERS = 2
HEADS = 16
HEAD_W = 128
DIFF_HEAD_DIM = 64
MLA_NOPE = 128
MLA_ROPE = 64
MLA_Q_LORA = 448
MLA_KV_LORA = 512
MLA_QK_W = 256
DIFF_SCALE = DIFF_HEAD_DIM ** -0.5
MLA_SCALE = (MLA_NOPE + MLA_ROPE) ** -0.5
ROPE_THETA = 10000.0
ROPE_QUARTER = 16
EPS = 1e-6
LOG2E = math.log2(math.e)
SUM_ROWS = 16

ROW_TILE = 512
MLA_ROW_TILE = 256
ADA_COL_TILE = 1024
QKV_COL_TILE = 1024
FF_TILE = 1024
DIFF_Q_TILE = 1024
MLA_Q_TILE = 2048
KEY_TILE = 512
VMEM_LIMIT = 58 * 1024 * 1024


def _cparams(sem):
    return pltpu.CompilerParams(dimension_semantics=sem, vmem_limit_bytes=VMEM_LIMIT)


def _diff_lambda_init(layer):
    return 0.8 - 0.6 * math.exp(-0.3 * layer)


def _ada_kernel(s_ref, w_ref, b_ref, o_ref):
    s = s_ref[...]
    s = s * (1.0 / (1.0 + jnp.exp(-s)))
    o_ref[0] = jnp.dot(s, w_ref[0], preferred_element_type=F32,
                       precision=lax.Precision.HIGHEST) + b_ref[0]


def _ada_all(cond, ada_w, ada_b):
    depth, d, n = ada_w.shape
    return pl.pallas_call(
        _ada_kernel,
        grid=(depth, n // ADA_COL_TILE),
        in_specs=[
            pl.BlockSpec((8, d), lambda i, j: (0, 0)),
            pl.BlockSpec((1, d, ADA_COL_TILE), lambda i, j: (i, 0, j)),
            pl.BlockSpec((1, 1, ADA_COL_TILE), lambda i, j: (i, 0, j)),
        ],
        out_specs=pl.BlockSpec((1, 8, ADA_COL_TILE), lambda i, j: (i, 0, j)),
        out_shape=jax.ShapeDtypeStruct((depth, 8, n), F32),
        compiler_params=_cparams(("parallel", "parallel")),
        name="ada",
    )(cond, ada_w, ada_b.reshape(depth, 1, n))


def _norm_mod(x, g, shift, scale):
    ms = jnp.mean(x * x, axis=-1, keepdims=True)
    y = x * lax.rsqrt(ms + EPS) * g
    return y * (1.0 + scale) + shift


def _rope(a, cos, sin_lo, sin_hi):
    up = pltpu.roll(a, HEAD_W - ROPE_QUARTER, 1)
    dn = pltpu.roll(a, ROPE_QUARTER, 1)
    return a * cos + up * sin_lo + dn * sin_hi


def _rope_tables(seq, pad_rows):
    t = jnp.arange(seq)
    row = (t // GRID_W).astype(F32)[:, None]
    col = (t % GRID_W).astype(F32)[:, None]
    inv = ROPE_THETA ** (-jnp.arange(ROPE_QUARTER, dtype=F32) / ROPE_QUARTER)
    ang = jnp.concatenate([row * inv, row * inv, col * inv, col * inv], axis=1)
    cos, sin = jnp.cos(ang), jnp.sin(ang)
    first = (jnp.arange(4 * ROPE_QUARTER) % (2 * ROPE_QUARTER)) < ROPE_QUARTER
    sin_lo = jnp.where(first, -sin, 0.0)
    sin_hi = jnp.where(first, 0.0, sin)

    def finish(tab, fill):
        tab = jnp.concatenate([tab, tab], axis=1)
        return jnp.concatenate([tab, jnp.full((pad_rows, HEAD_W), fill, F32)], axis=0)

    return finish(cos, 1.0), finish(sin_lo, 0.0), finish(sin_hi, 0.0)


def _qkv_kernel(x_ref, g_ref, mod_ref, w_ref, cos_ref, slo_ref, shi_ref, o_ref, h_ref, acc_ref, *,
                n_col, q_tiles, n_steps):
    t = pl.program_id(0)
    j = jnp.minimum(t, n_steps - 2) % n_col
    jp = jnp.maximum(t - 1, 0) % n_col

    @pl.when(t == 0)
    def _():
        acc_ref[...] = jnp.zeros_like(acc_ref)

    @pl.when(j == 0)
    def _():
        h = _norm_mod(x_ref[...], g_ref[...], mod_ref[0, 0:1, :], mod_ref[0, 1:2, :])
        h_ref[...] = h.astype(BF16)

    sc = jnp.where(jp < q_tiles, DIFF_SCALE * LOG2E, 1.0).astype(F32)
    cos, slo, shi = cos_ref[...], slo_ref[...], shi_ref[...]
    for c in range(acc_ref.shape[1] // HEAD_W):
        a = acc_ref[:, c * HEAD_W:(c + 1) * HEAD_W]
        o_ref[:, c * HEAD_W:(c + 1) * HEAD_W] = (_rope(a, cos, slo, shi) * sc).astype(BF16)
    acc_ref[...] = jnp.dot(h_ref[...], w_ref[...], preferred_element_type=F32)


def _qkv_proj(xa, g, mod, w, tabs, *, tiles_per_batch, n_lat_tiles, n_batch):
    rows, d = xa.shape
    n = w.shape[1]
    tm, tn = ROW_TILE, QKV_COL_TILE
    n_col = n // tn
    n_steps = (rows // tm) * n_col + 1
    rope_tiles = 2 * d // tn
    cur = lambda t: jnp.minimum(t, n_steps - 2)
    prev = lambda t: jnp.maximum(t - 1, 0)

    def tab_map(t):
        ip, jp = prev(t) // n_col, prev(t) % n_col
        return (jnp.where((ip < n_lat_tiles) & (jp < rope_tiles), ip % tiles_per_batch, tiles_per_batch), 0)

    tab_spec = pl.BlockSpec((tm, HEAD_W), tab_map)
    kern = functools.partial(_qkv_kernel, n_col=n_col, q_tiles=d // tn, n_steps=n_steps)
    return pl.pallas_call(
        kern,
        grid=(n_steps,),
        in_specs=[
            pl.BlockSpec((tm, d), lambda t: (cur(t) // n_col, 0)),
            pl.BlockSpec((1, d), lambda t: (0, 0)),
            pl.BlockSpec((1, 6, d), lambda t: (jnp.minimum(cur(t) // n_col // tiles_per_batch, n_batch), 0, 0)),
            pl.BlockSpec((d, tn), lambda t: (0, cur(t) % n_col)),
            tab_spec, tab_spec, tab_spec,
        ],
        out_specs=pl.BlockSpec((tm, tn), lambda t: (prev(t) // n_col, prev(t) % n_col)),
        out_shape=jax.ShapeDtypeStruct((rows, n), BF16),
        scratch_shapes=[pltpu.VMEM((tm, d), BF16), pltpu.VMEM((tm, tn), F32)],
        compiler_params=_cparams(("arbitrary",)),
        name="diff_qkv",
    )(xa, g, mod, w, *tabs)


def _mla_proj_kernel(x_ref, g_ref, mod_ref, wd_ref, qg_ref, kvg_ref, wuq_ref, wukn_ref, wuv_ref,
                     cos_ref, slo_ref, shi_ref, q_ref, k_ref, v_ref):
    h = _norm_mod(x_ref[...], g_ref[...], mod_ref[0, 0:1, :], mod_ref[0, 1:2, :]).astype(BF16)
    a = jnp.dot(h, wd_ref[...], preferred_element_type=F32)
    cq = a[:, :512]
    ckv = a[:, 512:1024]
    kpe = a[:, 1024:1152]
    cqn = cq * lax.rsqrt(jnp.sum(cq * cq, axis=-1, keepdims=True) * (1.0 / MLA_Q_LORA) + EPS) * qg_ref[...]
    ckvn = ckv * lax.rsqrt(jnp.mean(ckv * ckv, axis=-1, keepdims=True) + EPS) * kvg_ref[...]
    cqb = (cqn * (MLA_SCALE * LOG2E)).astype(BF16)
    ckvb = ckvn.astype(BF16)
    cos, slo, shi = cos_ref[...], slo_ref[...], shi_ref[...]

    kpe_r = _rope(kpe, cos, slo, shi).astype(BF16)
    group = 4
    for hg in range(HEADS // group):
        qc = jnp.dot(cqb, wuq_ref[:, hg * group * MLA_QK_W:(hg + 1) * group * MLA_QK_W],
                     preferred_element_type=F32)
        kc = jnp.dot(ckvb, wukn_ref[:, hg * group * HEAD_W:(hg + 1) * group * HEAD_W],
                     preferred_element_type=F32)
        for hh in range(group):
            o = (hg * group + hh) * MLA_QK_W
            q_ref[:, o:o + HEAD_W] = qc[:, hh * MLA_QK_W:hh * MLA_QK_W + HEAD_W].astype(BF16)
            qp = qc[:, hh * MLA_QK_W + HEAD_W:(hh + 1) * MLA_QK_W]
            q_ref[:, o + HEAD_W:o + MLA_QK_W] = _rope(qp, cos, slo, shi).astype(BF16)
            k_ref[:, o:o + HEAD_W] = kc[:, hh * HEAD_W:(hh + 1) * HEAD_W].astype(BF16)
            k_ref[:, o + HEAD_W:o + MLA_QK_W] = kpe_r
    v_ref[...] = jnp.dot(ckvb, wuv_ref[...], preferred_element_type=F32).astype(BF16)


def _mla_proj(xa, g, mod, wd, qg, kvg, wuq, wukn, wuv, tabs, *, seq, n_batch):
    rows, d = xa.shape
    tm = MLA_ROW_TILE
    tiles_per_batch = seq // tm
    n_lat_tiles = n_batch * tiles_per_batch
    full = lambda arr: pl.BlockSpec(arr.shape, lambda i: (0, 0))
    tab_spec = pl.BlockSpec(
        (tm, HEAD_W), lambda i: (jnp.where(i < n_lat_tiles, i % tiles_per_batch, tiles_per_batch), 0))
    return pl.pallas_call(
        _mla_proj_kernel,
        grid=(rows // tm,),
        in_specs=[
            pl.BlockSpec((tm, d), lambda i: (i, 0)),
            full(g),
            pl.BlockSpec((1, 6, d), lambda i: (jnp.minimum(i // tiles_per_batch, n_batch), 0, 0)),
            full(wd), full(qg), full(kvg), full(wuq), full(wukn), full(wuv),
            tab_spec, tab_spec, tab_spec,
        ],
        out_specs=[
            pl.BlockSpec((tm, HEADS * MLA_QK_W), lambda i: (i, 0)),
            pl.BlockSpec((tm, HEADS * MLA_QK_W), lambda i: (i, 0)),
            pl.BlockSpec((tm, HEADS * HEAD_W), lambda i: (i, 0)),
        ],
        out_shape=[
            jax.ShapeDtypeStruct((rows, HEADS * MLA_QK_W), BF16),
            jax.ShapeDtypeStruct((rows, HEADS * MLA_QK_W), BF16),
            jax.ShapeDtypeStruct((rows, HEADS * HEAD_W), BF16),
        ],
        compiler_params=_cparams(("parallel",)),
        name="mla_proj",
    )(xa, g, mod, wd, qg, kvg, wuq, wukn, wuv, *tabs)


def _attn_kernel(*refs, diff, has_lat, key_tile, lam_init):
    refs = list(refs)
    q_ref, kc_ref, vc_ref = refs[:3]
    pos = 3
    if has_lat:
        kl_ref, vl_ref = refs[pos:pos + 2]
        pos += 2
    if diff:
        lam_ref, sg_ref = refs[pos:pos + 2]
        pos += 2
    o_ref, m_ref, acc_ref = refs[pos:pos + 3]
    pos += 3
    if has_lat:
        s_ref, p_ref, a_ref, c_ref = refs[pos:pos + 4]

    q = q_ref[...]
    tq = q.shape[0]
    if diff:
        lane = lax.broadcasted_iota(jnp.int32, q.shape, 1)
        zero = jnp.zeros_like(q)
        q = jnp.concatenate([jnp.where(lane < DIFF_HEAD_DIM, q, zero),
                             jnp.where(lane >= DIFF_HEAD_DIM, q, zero)], axis=0)

    def scores(k):
        return lax.dot_general(k, q, (((1,), (1,)), ((), ())), preferred_element_type=F32)

    def weighted(vt, p):
        lhs = jnp.concatenate([vt, jnp.ones((SUM_ROWS, vt.shape[1]), BF16)], axis=0)
        return jnp.dot(lhs, p, preferred_element_type=F32)

    s = scores(kc_ref[...])
    m0 = jnp.max(s, axis=0, keepdims=True)
    m_ref[...] = m0
    acc_ref[...] = weighted(vc_ref[...], jnp.exp2(s - m0).astype(BF16))

    if has_lat:
        n_chunks = kl_ref.shape[0] // key_tile

        def chunk(t):
            return pl.ds(pl.multiple_of(t * key_tile, key_tile), key_tile)

        def put(ref, val):
            for c in range(ref.shape[0]):
                ref[c] = val[:, c * HEAD_W:(c + 1) * HEAD_W]

        def get(ref):
            return jnp.concatenate([ref[c] for c in range(ref.shape[0])], axis=1)

        def qk(t):
            s = scores(kl_ref[chunk(t), :])
            put(s_ref, s)
            c_ref[...] = jnp.max(s, axis=0, keepdims=True)

        def sm():
            m_prev = m_ref[...]
            m_new = jnp.maximum(m_prev, c_ref[...])
            a_ref[...] = jnp.exp2(m_prev - m_new)
            put(p_ref, jnp.exp2(get(s_ref) - m_new).astype(BF16))
            m_ref[...] = m_new

        def pv(t):
            acc_ref[...] = a_ref[...] * acc_ref[...] + weighted(vl_ref[:, chunk(t)], get(p_ref))

        qk(0)
        sm()
        qk(1)

        def body(t, carry):
            pv(t - 1)
            sm()
            qk(t + 1)
            return carry

        lax.fori_loop(1, n_chunks - 1, body, 0, unroll=2)
        pv(n_chunks - 2)
        sm()
        pv(n_chunks - 1)

    acc = acc_ref[...]
    inv_l = 1.0 / acc[HEAD_W:HEAD_W + 1, :]
    acc = acc[:HEAD_W, :]
    if diff:
        lf = lam_ref[...]
        lam = (jnp.exp(jnp.sum(lf[0:1] * lf[1:2], axis=-1, keepdims=True))
               - jnp.exp(jnp.sum(lf[2:3] * lf[3:4], axis=-1, keepdims=True)) + lam_init)
        o = acc[:, :tq] * inv_l[:, :tq] - lam * (acc[:, tq:] * inv_l[:, tq:])
        ms = jnp.mean(o * o, axis=0, keepdims=True)
        o = o * lax.rsqrt(ms + EPS) * sg_ref[...] * (1.0 - lam_init)
    else:
        o = acc * inv_l
    o_ref[...] = o.T.astype(BF16)


def _attention(q_arr, k_arr, vt_arr, *, diff, seq, n_batch, n_ctx, q_col, k_col, qk_w,
               lat_queries, lam=None, subln_g=None, lam_init=0.0, prev_out=None):
    rows = q_arr.shape[0]
    ctx_row0 = n_batch * seq
    if lat_queries:
        tq = DIFF_Q_TILE if diff else MLA_Q_TILE
        n_q = seq // tq
        q_row = lambda b, qi: b * n_q + qi
    else:
        tq = n_ctx
        n_q = 1
        q_row = lambda b, qi: ctx_row0 // n_ctx + b
    nq_cols = 2 * tq if diff else tq

    in_specs = [
        pl.BlockSpec((tq, qk_w), lambda b, h, qi: (q_row(b, qi), q_col + h)),
        pl.BlockSpec((n_ctx, qk_w), lambda b, h, qi: (ctx_row0 // n_ctx + b, k_col + h)),
        pl.BlockSpec((HEAD_W, n_ctx), lambda b, h, qi: (h, ctx_row0 // n_ctx + b)),
    ]
    args = [q_arr, k_arr, vt_arr]
    scratch = [pltpu.VMEM((1, nq_cols), F32), pltpu.VMEM((HEAD_W + SUM_ROWS, nq_cols), F32)]
    if lat_queries:
        in_specs += [
            pl.BlockSpec((seq, qk_w), lambda b, h, qi: (b, k_col + h)),
            pl.BlockSpec((HEAD_W, seq), lambda b, h, qi: (h, b)),
        ]
        args += [k_arr, vt_arr]
        scratch += [pltpu.VMEM((nq_cols // HEAD_W, KEY_TILE, HEAD_W), F32),
                    pltpu.VMEM((nq_cols // HEAD_W, KEY_TILE, HEAD_W), BF16),
                    pltpu.VMEM((1, nq_cols), F32), pltpu.VMEM((1, nq_cols), F32)]
    if diff:
        in_specs += [
            pl.BlockSpec(lam.shape, lambda b, h, qi: (0, 0)),
            pl.BlockSpec(subln_g.shape, lambda b, h, qi: (0, 0)),
        ]
        args += [lam, subln_g]
    aliases = {}
    if prev_out is not None:
        in_specs.append(pl.BlockSpec(memory_space=pl.ANY))
        args.append(prev_out)
        aliases = {len(args) - 1: 0}

    kern = functools.partial(_attn_kernel, diff=diff, has_lat=lat_queries, key_tile=KEY_TILE, lam_init=lam_init)
    if prev_out is not None:
        inner = kern
        kern = lambda *refs: inner(*refs[:len(args) - 1], *refs[len(args):])

    return pl.pallas_call(
        kern,
        grid=(n_batch, HEADS, n_q),
        in_specs=in_specs,
        out_specs=pl.BlockSpec((tq, HEAD_W), lambda b, h, qi: (q_row(b, qi), h)),
        out_shape=jax.ShapeDtypeStruct((rows, HEADS * HEAD_W), BF16),
        scratch_shapes=scratch,
        input_output_aliases=aliases,
        compiler_params=_cparams(("parallel", "parallel", "arbitrary")),
        name=("diff" if diff else "mla") + ("_attn_lat" if lat_queries else "_attn_ctx"),
    )(*args)


def _tail_kernel(o_ref, wo_ref, x_ref, g_ref, mod_ref, w1_ref, w2_ref, fg_ref, y_ref, h_ref, acc_ref, *, final):
    j = pl.program_id(1)

    @pl.when(j == 0)
    def _():
        x1 = x_ref[...] + mod_ref[0, 2:3, :] * jnp.dot(o_ref[...], wo_ref[...], preferred_element_type=F32)
        y_ref[...] = x1
        h = _norm_mod(x1, g_ref[...], mod_ref[0, 3:4, :], mod_ref[0, 4:5, :])
        h_ref[...] = h.astype(BF16)
        acc_ref[...] = jnp.zeros_like(acc_ref)

    a = jnp.dot(h_ref[...], w1_ref[...], preferred_element_type=F32)
    a = jnp.maximum(a, 0.0)
    acc_ref[...] += jnp.dot((a * a).astype(BF16), w2_ref[...], preferred_element_type=F32)

    @pl.when(j == pl.num_programs(1) - 1)
    def _():
        y = y_ref[...] + mod_ref[0, 5:6, :] * acc_ref[...]
        if final:
            ms = jnp.mean(y * y, axis=-1, keepdims=True)
            y = y * lax.rsqrt(ms + EPS) * fg_ref[...]
        y_ref[...] = y


def _tail(o, wo, xa, g, mod, w1, w2, final_g, *, n_tiles, tiles_per_batch, n_batch, final):
    d = xa.shape[1]
    ff = w1.shape[1]
    tm, tf = ROW_TILE, FF_TILE
    once = pl.Buffered(1)
    return pl.pallas_call(
        functools.partial(_tail_kernel, final=final),
        grid=(n_tiles, ff // tf),
        in_specs=[
            pl.BlockSpec((tm, o.shape[1]), lambda i, j: (i, 0)),
            pl.BlockSpec(wo.shape, lambda i, j: (0, 0), pipeline_mode=once),
            pl.BlockSpec((tm, d), lambda i, j: (i, 0)),
            pl.BlockSpec((1, d), lambda i, j: (0, 0)),
            pl.BlockSpec((1, 6, d), lambda i, j: (jnp.minimum(i // tiles_per_batch, n_batch), 0, 0)),
            pl.BlockSpec((d, tf), lambda i, j: (0, j)),
            pl.BlockSpec((tf, d), lambda i, j: (j, 0)),
            pl.BlockSpec((1, d), lambda i, j: (0, 0)),
        ],
        out_specs=pl.BlockSpec((tm, d), lambda i, j: (i, 0)),
        out_shape=jax.ShapeDtypeStruct((n_tiles * tm, d), F32),
        scratch_shapes=[pltpu.VMEM((tm, d), BF16), pltpu.VMEM((tm, d), F32)],
        compiler_params=_cparams(("parallel", "arbitrary")),
        name="tail",
    )(o, wo, xa, g, mod, w1, w2, final_g)


def _mla_weights(wdown, q_norm_g, wuq, kv_norm_g, wukv):
    d = wdown.shape[0]
    qpad = 512 - MLA_Q_LORA
    z = lambda n: jnp.zeros((d, n), wdown.dtype)
    kv_end = MLA_Q_LORA + MLA_KV_LORA
    wd = jnp.concatenate([wdown[:, :MLA_Q_LORA], z(qpad), wdown[:, MLA_Q_LORA:kv_end],
                          wdown[:, kv_end:], z(HEAD_W - MLA_ROPE)], axis=1).astype(BF16)
    qg = jnp.pad(q_norm_g, (0, qpad)).reshape(1, -1)
    kvg = kv_norm_g.reshape(1, -1)
    wq = wuq.reshape(MLA_Q_LORA, HEADS, MLA_NOPE + MLA_ROPE)
    wq = jnp.pad(wq, ((0, qpad), (0, 0), (0, MLA_QK_W - MLA_NOPE - MLA_ROPE)))
    wq = wq.reshape(512, HEADS * MLA_QK_W).astype(BF16)
    wkv = wukv.reshape(MLA_KV_LORA, HEADS, 2 * HEAD_W)
    wukn = wkv[:, :, :HEAD_W].reshape(MLA_KV_LORA, HEADS * HEAD_W).astype(BF16)
    wuv = wkv[:, :, HEAD_W:].reshape(MLA_KV_LORA, HEADS * HEAD_W).astype(BF16)
    return wd, qg, kvg, wq, wukn, wuv


def kernel(x, c, ctx, c_ctx, ada_w, ada_b, norm_mix_g, norm_mlp_g, dif_wqkv, dif_wo, dif_lambda, dif_subln_g,
           mla_wdown, mla_q_norm_g, mla_wuq, mla_kv_norm_g, mla_wukv, mla_wo, mlp_w1, mlp_w2, final_g):
    n_batch, seq, d = x.shape
    n_ctx = ctx.shape[1]
    depth = ada_w.shape[0]
    assert d == HEADS * HEAD_W and seq % ROW_TILE == 0 and (n_batch * n_ctx) % ROW_TILE == 0
    assert seq % (2 * KEY_TILE) == 0 and n_batch + 1 <= 8 and MLA_ROW_TILE == n_ctx

    tiles_per_batch = seq // ROW_TILE
    n_lat_tiles = n_batch * tiles_per_batch
    n_all_tiles = n_lat_tiles + n_batch * n_ctx // ROW_TILE
    geom = dict(tiles_per_batch=tiles_per_batch, n_batch=n_batch)

    cond = jnp.concatenate([c, c_ctx[None, :], jnp.zeros((8 - n_batch - 1, d), F32)], axis=0)
    mods = _ada_all(cond, ada_w, ada_b)[:, :n_batch + 1].reshape(depth, n_batch + 1, 6, d)

    tabs = _rope_tables(seq, ROW_TILE)
    xa = jnp.concatenate([x.reshape(n_batch * seq, d), ctx.reshape(n_batch * n_ctx, d)], axis=0)
    row = lambda v: v.reshape(1, -1)

    for i in range(depth):
        last = i == depth - 1
        j = i // N_MIXERS
        mod = mods[i]
        att = dict(seq=seq, n_batch=n_batch, n_ctx=n_ctx)
        if i % N_MIXERS == 0:
            qkv = _qkv_proj(xa, row(norm_mix_g[i]), mod, dif_wqkv[j].astype(BF16), tabs,
                            n_lat_tiles=n_lat_tiles, **geom)
            att.update(diff=True, q_col=0, k_col=HEADS, qk_w=HEAD_W, lam=dif_lambda[j],
                       subln_g=dif_subln_g[j].reshape(-1, 1), lam_init=_diff_lambda_init(i))
            vt = qkv[:, 2 * d:].T
            o = _attention(qkv, qkv, vt, lat_queries=True, **att)
            if not last:
                o = _attention(qkv, qkv, vt, lat_queries=False, prev_out=o, **att)
            wo = dif_wo[j]
        else:
            wts = _mla_weights(mla_wdown[j], mla_q_norm_g[j], mla_wuq[j], mla_kv_norm_g[j], mla_wukv[j])
            qa, ka, va = _mla_proj(xa, row(norm_mix_g[i]), mod, *wts, tabs, seq=seq, n_batch=n_batch)
            att.update(diff=False, q_col=0, k_col=0, qk_w=MLA_QK_W)
            vt = va.T
            o = _attention(qa, ka, vt, lat_queries=True, **att)
            if not last:
                o = _attention(qa, ka, vt, lat_queries=False, prev_out=o, **att)
            wo = mla_wo[j]
        n_tiles = n_lat_tiles if last else n_all_tiles
        xa = _tail(o, wo.astype(BF16), xa, row(norm_mlp_g[i]), mod, mlp_w1[i].astype(BF16), mlp_w2[i].astype(BF16),
                   row(final_g), n_tiles=n_tiles, final=last, **geom)

    return xa.reshape(n_batch, seq, d)
```

```python
import functools
import math

import jax
import jax.numpy as jnp
from jax import lax
from jax.experimental import pallas as pl
from jax.experimental.pallas import tpu as pltpu

F32 = jnp.float32
BF16 = jnp.bfloat16

GRID_W = 64
N_MIXERS = 2
HEADS = 16
HEAD_W = 128
DIFF_HEAD_DIM = 64
MLA_NOPE = 128
MLA_ROPE = 64
MLA_Q_LORA = 448
MLA_KV_LORA = 512
MLA_QK_W = 256
DIFF_SCALE = DIFF_HEAD_DIM ** -0.5
MLA_SCALE = (MLA_NOPE + MLA_ROPE) ** -0.5
ROPE_THETA = 10000.0
ROPE_QUARTER = 16
EPS = 1e-6
LOG2E = math.log2(math.e)
SPACER_BYTES = 8192
SUM_ROWS = 16

ROW_TILE = 512
MLA_ROW_TILE = 256
ADA_COL_TILE = 1024
QKV_COL_TILE = 1024
FF_TILE = 1024
DIFF_Q_TILE = 1024
MLA_Q_TILE = 2048
KEY_TILE = 512
VMEM_LIMIT = 58 * 1024 * 1024


def _cparams(sem):
    return pltpu.CompilerParams(dimension_semantics=sem, vmem_limit_bytes=VMEM_LIMIT)


def _diff_lambda_init(layer):
    return 0.8 - 0.6 * math.exp(-0.3 * layer)


def _ada_kernel(s_ref, w_ref, b_ref, o_ref):
    s = s_ref[...]
    s = s * (1.0 / (1.0 + jnp.exp(-s)))
    o_ref[0] = jnp.dot(s, w_ref[0], preferred_element_type=F32,
                       precision=lax.Precision.HIGHEST) + b_ref[0]


def _ada_all(cond, ada_w, ada_b):
    depth, d, n = ada_w.shape
    return pl.pallas_call(
        _ada_kernel,
        grid=(depth, n // ADA_COL_TILE),
        in_specs=[
            pl.BlockSpec((8, d), lambda i, j: (0, 0)),
            pl.BlockSpec((1, d, ADA_COL_TILE), lambda i, j: (i, 0, j)),
            pl.BlockSpec((1, 1, ADA_COL_TILE), lambda i, j: (i, 0, j)),
        ],
        out_specs=pl.BlockSpec((1, 8, ADA_COL_TILE), lambda i, j: (i, 0, j)),
        out_shape=jax.ShapeDtypeStruct((depth, 8, n), F32),
        compiler_params=_cparams(("parallel", "parallel")),
        name="ada",
    )(cond, ada_w, ada_b.reshape(depth, 1, n))


def _norm_mod(x, g, shift, scale):
    ms = jnp.mean(x * x, axis=-1, keepdims=True)
    y = x * lax.rsqrt(ms + EPS) * g
    return y * (1.0 + scale) + shift


def _rope(a, cos, sin_lo, sin_hi):
    up = pltpu.roll(a, HEAD_W - ROPE_QUARTER, 1)
    dn = pltpu.roll(a, ROPE_QUARTER, 1)
    return a * cos + up * sin_lo + dn * sin_hi


def _rope_tables(seq, pad_rows):
    t = jnp.arange(seq)
    row = (t // GRID_W).astype(F32)[:, None]
    col = (t % GRID_W).astype(F32)[:, None]
    inv = ROPE_THETA ** (-jnp.arange(ROPE_QUARTER, dtype=F32) / ROPE_QUARTER)
    ang = jnp.concatenate([row * inv, row * inv, col * inv, col * inv], axis=1)
    cos, sin = jnp.cos(ang), jnp.sin(ang)
    first = (jnp.arange(4 * ROPE_QUARTER) % (2 * ROPE_QUARTER)) < ROPE_QUARTER
    sin_lo = jnp.where(first, -sin, 0.0)
    sin_hi = jnp.where(first, 0.0, sin)

    def finish(tab, fill):
        tab = jnp.concatenate([tab, tab], axis=1)
        return jnp.concatenate([tab, jnp.full((pad_rows, HEAD_W), fill, F32)], axis=0)

    return finish(cos, 1.0), finish(sin_lo, 0.0), finish(sin_hi, 0.0)


def _qkv_kernel(x_ref, g_ref, mod_ref, w_ref, cos_ref, slo_ref, shi_ref, o_ref, h_ref, acc_ref, *,
                n_col, q_tiles, n_steps):
    t = pl.program_id(0)
    j = jnp.minimum(t, n_steps - 2) % n_col
    jp = jnp.maximum(t - 1, 0) % n_col

    @pl.when(t == 0)
    def _():
        acc_ref[...] = jnp.zeros_like(acc_ref)

    @pl.when(j == 0)
    def _():
        h = _norm_mod(x_ref[...], g_ref[...], mod_ref[0, 0:1, :], mod_ref[0, 1:2, :])
        h_ref[...] = h.astype(BF16)

    sc = jnp.where(jp < q_tiles, DIFF_SCALE * LOG2E, 1.0).astype(F32)
    cos, slo, shi = cos_ref[...], slo_ref[...], shi_ref[...]
    for c in range(acc_ref.shape[1] // HEAD_W):
        a = acc_ref[:, c * HEAD_W:(c + 1) * HEAD_W]
        o_ref[:, c * HEAD_W:(c + 1) * HEAD_W] = (_rope(a, cos, slo, shi) * sc).astype(BF16)
    acc_ref[...] = jnp.dot(h_ref[...], w_ref[...], preferred_element_type=F32)


def _qkv_proj(xa, g, mod, w, tabs, *, tiles_per_batch, n_lat_tiles, n_batch):
    rows, d = xa.shape
    n = w.shape[1]
    tm, tn = ROW_TILE, QKV_COL_TILE
    n_col = n // tn
    n_steps = (rows // tm) * n_col + 1
    rope_tiles = 2 * d // tn
    cur = lambda t: jnp.minimum(t, n_steps - 2)
    prev = lambda t: jnp.maximum(t - 1, 0)

    def tab_map(t):
        ip, jp = prev(t) // n_col, prev(t) % n_col
        return (jnp.where((ip < n_lat_tiles) & (jp < rope_tiles), ip % tiles_per_batch, tiles_per_batch), 0)

    tab_spec = pl.BlockSpec((tm, HEAD_W), tab_map)
    kern = functools.partial(_qkv_kernel, n_col=n_col, q_tiles=d // tn, n_steps=n_steps)
    return pl.pallas_call(
        kern,
        grid=(n_steps,),
        in_specs=[
            pl.BlockSpec((tm, d), lambda t: (cur(t) // n_col, 0)),
            pl.BlockSpec((1, d), lambda t: (0, 0)),
            pl.BlockSpec((1, 6, d), lambda t: (jnp.minimum(cur(t) // n_col // tiles_per_batch, n_batch), 0, 0)),
            pl.BlockSpec((d, tn), lambda t: (0, cur(t) % n_col)),
            tab_spec, tab_spec, tab_spec,
        ],
        out_specs=pl.BlockSpec((tm, tn), lambda t: (prev(t) // n_col, prev(t) % n_col)),
        out_shape=jax.ShapeDtypeStruct((rows, n), BF16),
        scratch_shapes=[pltpu.VMEM((tm, d), BF16), pltpu.VMEM((tm, tn), F32)],
        compiler_params=_cparams(("arbitrary",)),
        name="diff_qkv",
    )(xa, g, mod, w, *tabs)


def _mla_proj_kernel(x_ref, g_ref, mod_ref, wd_ref, qg_ref, kvg_ref, wuq_ref, wukn_ref, wuv_ref,
                     cos_ref, slo_ref, shi_ref, q_ref, k_ref, v_ref):
    h = _norm_mod(x_ref[...], g_ref[...], mod_ref[0, 0:1, :], mod_ref[0, 1:2, :]).astype(BF16)
    a = jnp.dot(h, wd_ref[...], preferred_element_type=F32)
    cq = a[:, :512]
    ckv = a[:, 512:1024]
    kpe = a[:, 1024:1152]
    cqn = cq * lax.rsqrt(jnp.sum(cq * cq, axis=-1, keepdims=True) * (1.0 / MLA_Q_LORA) + EPS) * qg_ref[...]
    ckvn = ckv * lax.rsqrt(jnp.mean(ckv * ckv, axis=-1, keepdims=True) + EPS) * kvg_ref[...]
    cqb = (cqn * (MLA_SCALE * LOG2E)).astype(BF16)
    ckvb = ckvn.astype(BF16)
    cos, slo, shi = cos_ref[...], slo_ref[...], shi_ref[...]

    kpe_r = _rope(kpe, cos, slo, shi).astype(BF16)
    group = 4
    for hg in range(HEADS // group):
        qc = jnp.dot(cqb, wuq_ref[:, hg * group * MLA_QK_W:(hg + 1) * group * MLA_QK_W],
                     preferred_element_type=F32)
        kc = jnp.dot(ckvb, wukn_ref[:, hg * group * HEAD_W:(hg + 1) * group * HEAD_W],
                     preferred_element_type=F32)
        for hh in range(group):
            o = (hg * group + hh) * MLA_QK_W
            q_ref[:, o:o + HEAD_W] = qc[:, hh * MLA_QK_W:hh * MLA_QK_W + HEAD_W].astype(BF16)
            qp = qc[:, hh * MLA_QK_W + HEAD_W:(hh + 1) * MLA_QK_W]
            q_ref[:, o + HEAD_W:o + MLA_QK_W] = _rope(qp, cos, slo, shi).astype(BF16)
            k_ref[:, o:o + HEAD_W] = kc[:, hh * HEAD_W:(hh + 1) * HEAD_W].astype(BF16)
            k_ref[:, o + HEAD_W:o + MLA_QK_W] = kpe_r
    v_ref[...] = jnp.dot(ckvb, wuv_ref[...], preferred_element_type=F32).astype(BF16)


def _mla_proj(xa, g, mod, wd, qg, kvg, wuq, wukn, wuv, tabs, *, seq, n_batch):
    rows, d = xa.shape
    tm = MLA_ROW_TILE
    tiles_per_batch = seq // tm
    n_lat_tiles = n_batch * tiles_per_batch
    full = lambda arr: pl.BlockSpec(arr.shape, lambda i: (0, 0))
    tab_spec = pl.BlockSpec(
        (tm, HEAD_W), lambda i: (jnp.where(i < n_lat_tiles, i % tiles_per_batch, tiles_per_batch), 0))
    return pl.pallas_call(
        _mla_proj_kernel,
        grid=(rows // tm,),
        in_specs=[
            pl.BlockSpec((tm, d), lambda i: (i, 0)),
            full(g),
            pl.BlockSpec((1, 6, d), lambda i: (jnp.minimum(i // tiles_per_batch, n_batch), 0, 0)),
            full(wd), full(qg), full(kvg), full(wuq), full(wukn), full(wuv),
            tab_spec, tab_spec, tab_spec,
        ],
        out_specs=[
            pl.BlockSpec((tm, HEADS * MLA_QK_W), lambda i: (i, 0)),
            pl.BlockSpec((tm, HEADS * MLA_QK_W), lambda i: (i, 0)),
            pl.BlockSpec((tm, HEADS * HEAD_W), lambda i: (i, 0)),
        ],
        out_shape=[
            jax.ShapeDtypeStruct((rows, HEADS * MLA_QK_W), BF16),
            jax.ShapeDtypeStruct((rows, HEADS * MLA_QK_W), BF16),
            jax.ShapeDtypeStruct((rows, HEADS * HEAD_W), BF16),
        ],
        compiler_params=_cparams(("parallel",)),
        name="mla_proj",
    )(xa, g, mod, wd, qg, kvg, wuq, wukn, wuv, *tabs)


def _attn_kernel(*refs, diff, has_lat, key_tile, lam_init):
    refs = list(refs)
    q_ref, kc_ref, vc_ref = refs[:3]
    pos = 3
    if has_lat:
        kl_ref, vl_ref = refs[pos:pos + 2]
        pos += 2
    if diff:
        lam_ref, sg_ref = refs[pos:pos + 2]
        pos += 2
    o_ref, m_ref, acc_ref = refs[pos:pos + 3]
    pos += 3
    if has_lat:
        n_blk = m_ref.shape[1] // HEAD_W
        s_refs = refs[pos:pos + n_blk]
        p_refs = refs[pos + n_blk:pos + 2 * n_blk]
        a_ref, c_ref = refs[pos + 2 * n_blk:pos + 2 * n_blk + 2]

    q = q_ref[...]
    tq = q.shape[0]
    if diff:
        lane = lax.broadcasted_iota(jnp.int32, q.shape, 1)
        zero = jnp.zeros_like(q)
        q = jnp.concatenate([jnp.where(lane < DIFF_HEAD_DIM, q, zero),
                             jnp.where(lane >= DIFF_HEAD_DIM, q, zero)], axis=0)

    def scores(k):
        return lax.dot_general(k, q, (((1,), (1,)), ((), ())), preferred_element_type=F32)

    def weighted(vt, p):
        lhs = jnp.concatenate([vt, jnp.ones((SUM_ROWS, vt.shape[1]), BF16)], axis=0)
        return jnp.dot(lhs, p, preferred_element_type=F32)

    s = scores(kc_ref[...])
    m0 = jnp.max(s, axis=0, keepdims=True)
    m_ref[...] = m0
    acc_ref[...] = weighted(vc_ref[...], jnp.exp2(s - m0).astype(BF16))

    if has_lat:
        n_chunks = kl_ref.shape[0] // key_tile

        def chunk(t):
            return pl.ds(pl.multiple_of(t * key_tile, key_tile), key_tile)

        def put(blocks, val):
            for c, ref in enumerate(blocks):
                ref[:key_tile, :] = val[:, c * HEAD_W:(c + 1) * HEAD_W]

        def get(blocks):
            return jnp.concatenate([ref[:key_tile, :] for ref in blocks], axis=1)

        def qk(t):
            s = scores(kl_ref[chunk(t), :])
            put(s_refs, s)
            c_ref[...] = jnp.max(s, axis=0, keepdims=True)

        def sm():
            m_prev = m_ref[...]
            m_new = jnp.maximum(m_prev, c_ref[...])
            a_ref[...] = jnp.exp2(m_prev - m_new)
            put(p_refs, jnp.exp2(get(s_refs) - m_new).astype(BF16))
            m_ref[...] = m_new

        def pv(t):
            acc_ref[...] = a_ref[...] * acc_ref[...] + weighted(vl_ref[:, chunk(t)], get(p_refs))

        qk(0)
        sm()
        qk(1)

        def body(t, carry):
            pv(t - 1)
            sm()
            qk(t + 1)
            return carry

        lax.fori_loop(1, n_chunks - 1, body, 0, unroll=2)
        pv(n_chunks - 2)
        sm()
        pv(n_chunks - 1)

    acc = acc_ref[...]
    inv_l = 1.0 / acc[HEAD_W:HEAD_W + 1, :]
    acc = acc[:HEAD_W, :]
    if diff:
        lf = lam_ref[...]
        lam = (jnp.exp(jnp.sum(lf[0:1] * lf[1:2], axis=-1, keepdims=True))
               - jnp.exp(jnp.sum(lf[2:3] * lf[3:4], axis=-1, keepdims=True)) + lam_init)
        o = acc[:, :tq] * inv_l[:, :tq] - lam * (acc[:, tq:] * inv_l[:, tq:])
        ms = jnp.mean(o * o, axis=0, keepdims=True)
        o = o * lax.rsqrt(ms + EPS) * sg_ref[...] * (1.0 - lam_init)
    else:
        o = acc * inv_l
    o_ref[...] = o.T.astype(BF16)


def _attention(q_arr, k_arr, vt_arr, *, diff, seq, n_batch, n_ctx, q_col, k_col, qk_w,
               lat_queries, lam=None, subln_g=None, lam_init=0.0, prev_out=None):
    rows = q_arr.shape[0]
    ctx_row0 = n_batch * seq
    if lat_queries:
        tq = DIFF_Q_TILE if diff else MLA_Q_TILE
        n_q = seq // tq
        q_row = lambda b, qi: b * n_q + qi
    else:
        tq = n_ctx
        n_q = 1
        q_row = lambda b, qi: ctx_row0 // n_ctx + b
    nq_cols = 2 * tq if diff else tq

    in_specs = [
        pl.BlockSpec((tq, qk_w), lambda b, h, qi: (q_row(b, qi), q_col + h)),
        pl.BlockSpec((n_ctx, qk_w), lambda b, h, qi: (ctx_row0 // n_ctx + b, k_col + h)),
        pl.BlockSpec((HEAD_W, n_ctx), lambda b, h, qi: (h, ctx_row0 // n_ctx + b)),
    ]
    args = [q_arr, k_arr, vt_arr]
    scratch = [pltpu.VMEM((1, nq_cols), F32), pltpu.VMEM((HEAD_W + SUM_ROWS, nq_cols), F32)]
    if lat_queries:
        in_specs += [
            pl.BlockSpec((seq, qk_w), lambda b, h, qi: (b, k_col + h)),
            pl.BlockSpec((HEAD_W, seq), lambda b, h, qi: (h, b)),
        ]
        args += [k_arr, vt_arr]
        n_blk = nq_cols // HEAD_W
        scratch += [pltpu.VMEM((KEY_TILE + SPACER_BYTES // (4 * HEAD_W), HEAD_W), F32)] * n_blk
        scratch += [pltpu.VMEM((KEY_TILE + SPACER_BYTES // (2 * HEAD_W), HEAD_W), BF16)] * n_blk
        scratch += [pltpu.VMEM((1, nq_cols), F32), pltpu.VMEM((1, nq_cols), F32)]
    if diff:
        in_specs += [
            pl.BlockSpec(lam.shape, lambda b, h, qi: (0, 0)),
            pl.BlockSpec(subln_g.shape, lambda b, h, qi: (0, 0)),
        ]
        args += [lam, subln_g]
    aliases = {}
    if prev_out is not None:
        in_specs.append(pl.BlockSpec(memory_space=pl.ANY))
        args.append(prev_out)
        aliases = {len(args) - 1: 0}

    kern = functools.partial(_attn_kernel, diff=diff, has_lat=lat_queries, key_tile=KEY_TILE, lam_init=lam_init)
    if prev_out is not None:
        inner = kern
        kern = lambda *refs: inner(*refs[:len(args) - 1], *refs[len(args):])

    return pl.pallas_call(
        kern,
        grid=(n_batch, HEADS, n_q),
        in_specs=in_specs,
        out_specs=pl.BlockSpec((tq, HEAD_W), lambda b, h, qi: (q_row(b, qi), h)),
        out_shape=jax.ShapeDtypeStruct((rows, HEADS * HEAD_W), BF16),
        scratch_shapes=scratch,
        input_output_aliases=aliases,
        compiler_params=_cparams(("parallel", "parallel", "arbitrary")),
        name=("diff" if diff else "mla") + ("_attn_lat" if lat_queries else "_attn_ctx"),
    )(*args)


def _tail_kernel(o_ref, wo_ref, x_ref, g_ref, mod_ref, w1_ref, w2_ref, fg_ref, y_ref, h_ref, acc_ref, *, final):
    j = pl.program_id(1)

    @pl.when(j == 0)
    def _():
        x1 = x_ref[...] + mod_ref[0, 2:3, :] * jnp.dot(o_ref[...], wo_ref[...], preferred_element_type=F32)
        y_ref[...] = x1
        h = _norm_mod(x1, g_ref[...], mod_ref[0, 3:4, :], mod_ref[0, 4:5, :])
        h_ref[...] = h.astype(BF16)
        acc_ref[...] = jnp.zeros_like(acc_ref)

    a = jnp.dot(h_ref[...], w1_ref[...], preferred_element_type=F32)
    a = jnp.maximum(a, 0.0)
    acc_ref[...] += jnp.dot((a * a).astype(BF16), w2_ref[...], preferred_element_type=F32)

    @pl.when(j == pl.num_programs(1) - 1)
    def _():
        y = y_ref[...] + mod_ref[0, 5:6, :] * acc_ref[...]
        if final:
            ms = jnp.mean(y * y, axis=-1, keepdims=True)
            y = y * lax.rsqrt(ms + EPS) * fg_ref[...]
        y_ref[...] = y


def _tail(o, wo, xa, g, mod, w1, w2, final_g, *, n_tiles, tiles_per_batch, n_batch, final):
    d = xa.shape[1]
    ff = w1.shape[1]
    tm, tf = ROW_TILE, FF_TILE
    once = pl.Buffered(1)
    return pl.pallas_call(
        functools.partial(_tail_kernel, final=final),
        grid=(n_tiles, ff // tf),
        in_specs=[
            pl.BlockSpec((tm, o.shape[1]), lambda i, j: (i, 0)),
            pl.BlockSpec(wo.shape, lambda i, j: (0, 0), pipeline_mode=once),
            pl.BlockSpec((tm, d), lambda i, j: (i, 0)),
            pl.BlockSpec((1, d), lambda i, j: (0, 0)),
            pl.BlockSpec((1, 6, d), lambda i, j: (jnp.minimum(i // tiles_per_batch, n_batch), 0, 0)),
            pl.BlockSpec((d, tf), lambda i, j: (0, j)),
            pl.BlockSpec((tf, d), lambda i, j: (j, 0)),
            pl.BlockSpec((1, d), lambda i, j: (0, 0)),
        ],
        out_specs=pl.BlockSpec((tm, d), lambda i, j: (i, 0)),
        out_shape=jax.ShapeDtypeStruct((n_tiles * tm, d), F32),
        scratch_shapes=[pltpu.VMEM((tm, d), BF16), pltpu.VMEM((tm, d), F32)],
        compiler_params=_cparams(("parallel", "arbitrary")),
        name="tail",
    )(o, wo, xa, g, mod, w1, w2, final_g)


def _mla_weights(wdown, q_norm_g, wuq, kv_norm_g, wukv):
    d = wdown.shape[0]
    qpad = 512 - MLA_Q_LORA
    z = lambda n: jnp.zeros((d, n), wdown.dtype)
    kv_end = MLA_Q_LORA + MLA_KV_LORA
    wd = jnp.concatenate([wdown[:, :MLA_Q_LORA], z(qpad), wdown[:, MLA_Q_LORA:kv_end],
                          wdown[:, kv_end:], z(HEAD_W - MLA_ROPE)], axis=1).astype(BF16)
    qg = jnp.pad(q_norm_g, (0, qpad)).reshape(1, -1)
    kvg = kv_norm_g.reshape(1, -1)
    wq = wuq.reshape(MLA_Q_LORA, HEADS, MLA_NOPE + MLA_ROPE)
    wq = jnp.pad(wq, ((0, qpad), (0, 0), (0, MLA_QK_W - MLA_NOPE - MLA_ROPE)))
    wq = wq.reshape(512, HEADS * MLA_QK_W).astype(BF16)
    wkv = wukv.reshape(MLA_KV_LORA, HEADS, 2 * HEAD_W)
    wukn = wkv[:, :, :HEAD_W].reshape(MLA_KV_LORA, HEADS * HEAD_W).astype(BF16)
    wuv = wkv[:, :, HEAD_W:].reshape(MLA_KV_LORA, HEADS * HEAD_W).astype(BF16)
    return wd, qg, kvg, wq, wukn, wuv


def kernel(x, c, ctx, c_ctx, ada_w, ada_b, norm_mix_g, norm_mlp_g, dif_wqkv, dif_wo, dif_lambda, dif_subln_g,
           mla_wdown, mla_q_norm_g, mla_wuq, mla_kv_norm_g, mla_wukv, mla_wo, mlp_w1, mlp_w2, final_g):
    n_batch, seq, d = x.shape
    n_ctx = ctx.shape[1]
    depth = ada_w.shape[0]
    assert d == HEADS * HEAD_W and seq % ROW_TILE == 0 and (n_batch * n_ctx) % ROW_TILE == 0
    assert seq % (2 * KEY_TILE) == 0 and n_batch + 1 <= 8 and MLA_ROW_TILE == n_ctx

    tiles_per_batch = seq // ROW_TILE
    n_lat_tiles = n_batch * tiles_per_batch
    n_all_tiles = n_lat_tiles + n_batch * n_ctx // ROW_TILE
    geom = dict(tiles_per_batch=tiles_per_batch, n_batch=n_batch)

    cond = jnp.concatenate([c, c_ctx[None, :], jnp.zeros((8 - n_batch - 1, d), F32)], axis=0)
    mods = _ada_all(cond, ada_w, ada_b)[:, :n_batch + 1].reshape(depth, n_batch + 1, 6, d)

    tabs = _rope_tables(seq, ROW_TILE)
    xa = jnp.concatenate([x.reshape(n_batch * seq, d), ctx.reshape(n_batch * n_ctx, d)], axis=0)
    row = lambda v: v.reshape(1, -1)

    for i in range(depth):
        last = i == depth - 1
        j = i // N_MIXERS
        mod = mods[i]
        att = dict(seq=seq, n_batch=n_batch, n_ctx=n_ctx)
        if i % N_MIXERS == 0:
            qkv = _qkv_proj(xa, row(norm_mix_g[i]), mod, dif_wqkv[j].astype(BF16), tabs,
                            n_lat_tiles=n_lat_tiles, **geom)
            att.update(diff=True, q_col=0, k_col=HEADS, qk_w=HEAD_W, lam=dif_lambda[j],
                       subln_g=dif_subln_g[j].reshape(-1, 1), lam_init=_diff_lambda_init(i))
            vt = qkv[:, 2 * d:].T
            o = _attention(qkv, qkv, vt, lat_queries=True, **att)
            if not last:
                o = _attention(qkv, qkv, vt, lat_queries=False, prev_out=o, **att)
            wo = dif_wo[j]
        else:
            wts = _mla_weights(mla_wdown[j], mla_q_norm_g[j], mla_wuq[j], mla_kv_norm_g[j], mla_wukv[j])
            qa, ka, va = _mla_proj(xa, row(norm_mix_g[i]), mod, *wts, tabs, seq=seq, n_batch=n_batch)
            att.update(diff=False, q_col=0, k_col=0, qk_w=MLA_QK_W)
            vt = va.T
            o = _attention(qa, ka, vt, lat_queries=True, **att)
            if not last:
                o = _attention(qa, ka, vt, lat_queries=False, prev_out=o, **att)
            wo = mla_wo[j]
        n_tiles = n_lat_tiles if last else n_all_tiles
        xa = _tail(o, wo.astype(BF16), xa, row(norm_mlp_g[i]), mod, mlp_w1[i].astype(BF16), mlp_w2[i].astype(BF16),
                   row(final_g), n_tiles=n_tiles, final=last, **geom)

    return xa.reshape(n_batch, seq, d)
```

```python
import functools
import math

import jax
import jax.numpy as jnp
from jax import lax
from jax.experimental import pallas as pl
from jax.experimental.pallas import tpu as pltpu

F32 = jnp.float32
BF16 = jnp.bfloat16

GRID_W = 64
N_MIXERS = 2
HEADS = 16
HEAD_W = 128
DIFF_HEAD_DIM = 64
MLA_NOPE = 128
MLA_ROPE = 64
MLA_Q_LORA = 448
MLA_KV_LORA = 512
MLA_Q_PAD = 512
MLA_QK_W = 256
DIFF_SCALE = DIFF_HEAD_DIM ** -0.5
MLA_SCALE = (MLA_NOPE + MLA_ROPE) ** -0.5
ROPE_THETA = 10000.0
ROPE_QUARTER = 16
EPS = 1e-6
LOG2E = math.log2(math.e)
SUM_ROWS = 16

ROW_TILE = 512
MLA_ROW_TILE = 256
ADA_COL_TILE = 1024
QKV_COL_TILE = 1024
FF_TILE = 1024
DIFF_Q_TILE = 1024
MLA_Q_TILE = 2048
KEY_TILE = 512
VMEM_LIMIT = 58 * 1024 * 1024


def _cparams(sem):
    return pltpu.CompilerParams(dimension_semantics=sem, vmem_limit_bytes=VMEM_LIMIT)


def _diff_lambda_init(layer):
    return 0.8 - 0.6 * math.exp(-0.3 * layer)


def _ada_kernel(s_ref, w_ref, b_ref, o_ref):
    s = s_ref[...]
    s = s * (1.0 / (1.0 + jnp.exp(-s)))
    o_ref[0] = jnp.dot(s, w_ref[0], preferred_element_type=F32,
                       precision=lax.Precision.HIGHEST) + b_ref[0]


def _ada_all(cond, ada_w, ada_b):
    depth, d, n = ada_w.shape
    return pl.pallas_call(
        _ada_kernel,
        grid=(depth, n // ADA_COL_TILE),
        in_specs=[
            pl.BlockSpec((8, d), lambda i, j: (0, 0)),
            pl.BlockSpec((1, d, ADA_COL_TILE), lambda i, j: (i, 0, j)),
            pl.BlockSpec((1, 1, ADA_COL_TILE), lambda i, j: (i, 0, j)),
        ],
        out_specs=pl.BlockSpec((1, 8, ADA_COL_TILE), lambda i, j: (i, 0, j)),
        out_shape=jax.ShapeDtypeStruct((depth, 8, n), F32),
        compiler_params=_cparams(("parallel", "parallel")),
        name="ada",
    )(cond, ada_w, ada_b.reshape(depth, 1, n))


def _norm_mod(x, g, shift, scale):
    ms = jnp.mean(x * x, axis=-1, keepdims=True)
    y = x * lax.rsqrt(ms + EPS) * g
    return y * (1.0 + scale) + shift


def _rope(a, cos, sin_lo, sin_hi):
    up = pltpu.roll(a, HEAD_W - ROPE_QUARTER, 1)
    dn = pltpu.roll(a, ROPE_QUARTER, 1)
    return a * cos + up * sin_lo + dn * sin_hi


def _rope_tables(seq, pad_rows):
    t = jnp.arange(seq)
    row = (t // GRID_W).astype(F32)[:, None]
    col = (t % GRID_W).astype(F32)[:, None]
    inv = ROPE_THETA ** (-jnp.arange(ROPE_QUARTER, dtype=F32) / ROPE_QUARTER)
    ang = jnp.concatenate([row * inv, row * inv, col * inv, col * inv], axis=1)
    cos, sin = jnp.cos(ang), jnp.sin(ang)
    first = (jnp.arange(4 * ROPE_QUARTER) % (2 * ROPE_QUARTER)) < ROPE_QUARTER
    sin_lo = jnp.where(first, -sin, 0.0)
    sin_hi = jnp.where(first, 0.0, sin)

    def finish(tab, fill):
        tab = jnp.concatenate([tab, tab], axis=1)
        return jnp.concatenate([tab, jnp.full((pad_rows, HEAD_W), fill, F32)], axis=0)

    return finish(cos, 1.0), finish(sin_lo, 0.0), finish(sin_hi, 0.0)


def _qkv_kernel(x_ref, g_ref, mod_ref, w_ref, cos_ref, slo_ref, shi_ref, o_ref, h_ref, acc_ref, *,
                n_col, q_tiles, n_steps):
    t = pl.program_id(0)
    j = jnp.minimum(t, n_steps - 2) % n_col
    jp = jnp.maximum(t - 1, 0) % n_col

    @pl.when(t == 0)
    def _():
        acc_ref[...] = jnp.zeros_like(acc_ref)

    @pl.when(j == 0)
    def _():
        h = _norm_mod(x_ref[...], g_ref[...], mod_ref[0, 0:1, :], mod_ref[0, 1:2, :])
        h_ref[...] = h.astype(BF16)

    sc = jnp.where(jp < q_tiles, DIFF_SCALE * LOG2E, 1.0).astype(F32)
    cos, slo, shi = cos_ref[...], slo_ref[...], shi_ref[...]
    for c in range(acc_ref.shape[1] // HEAD_W):
        a = acc_ref[:, c * HEAD_W:(c + 1) * HEAD_W]
        o_ref[:, c * HEAD_W:(c + 1) * HEAD_W] = (_rope(a, cos, slo, shi) * sc).astype(BF16)
    acc_ref[...] = jnp.dot(h_ref[...], w_ref[...], preferred_element_type=F32)


def _qkv_proj(xa, g, mod, w, tabs, *, tiles_per_batch, n_lat_tiles, n_batch):
    rows, d = xa.shape
    n = w.shape[1]
    tm, tn = ROW_TILE, QKV_COL_TILE
    n_col = n // tn
    n_steps = (rows // tm) * n_col + 1
    rope_tiles = 2 * d // tn
    cur = lambda t: jnp.minimum(t, n_steps - 2)
    prev = lambda t: jnp.maximum(t - 1, 0)

    def tab_map(t):
        ip, jp = prev(t) // n_col, prev(t) % n_col
        return (jnp.where((ip < n_lat_tiles) & (jp < rope_tiles), ip % tiles_per_batch, tiles_per_batch), 0)

    tab_spec = pl.BlockSpec((tm, HEAD_W), tab_map)
    kern = functools.partial(_qkv_kernel, n_col=n_col, q_tiles=d // tn, n_steps=n_steps)
    return pl.pallas_call(
        kern,
        grid=(n_steps,),
        in_specs=[
            pl.BlockSpec((tm, d), lambda t: (cur(t) // n_col, 0)),
            pl.BlockSpec((1, d), lambda t: (0, 0)),
            pl.BlockSpec((1, 6, d), lambda t: (jnp.minimum(cur(t) // n_col // tiles_per_batch, n_batch), 0, 0)),
            pl.BlockSpec((d, tn), lambda t: (0, cur(t) % n_col)),
            tab_spec, tab_spec, tab_spec,
        ],
        out_specs=pl.BlockSpec((tm, tn), lambda t: (prev(t) // n_col, prev(t) % n_col)),
        out_shape=jax.ShapeDtypeStruct((rows, n), BF16),
        scratch_shapes=[pltpu.VMEM((tm, d), BF16), pltpu.VMEM((tm, tn), F32)],
        compiler_params=_cparams(("arbitrary",)),
        name="diff_qkv",
    )(xa, g, mod, w, *tabs)


def _mla_proj_kernel(x_ref, g_ref, mod_ref, wd_ref, qg_ref, kvg_ref, wuq_ref, wukn_ref, wuv_ref,
                     cos_ref, slo_ref, shi_ref, q_ref, k_ref, v_ref):
    h = _norm_mod(x_ref[...], g_ref[...], mod_ref[0, 0:1, :], mod_ref[0, 1:2, :]).astype(BF16)
    a = jnp.dot(h, wd_ref[...], preferred_element_type=F32)
    cq = a[:, :MLA_Q_PAD]
    ckv = a[:, MLA_Q_PAD:MLA_Q_PAD + MLA_KV_LORA]
    kpe = a[:, MLA_Q_PAD + MLA_KV_LORA:]
    cqn = cq * lax.rsqrt(jnp.sum(cq * cq, axis=-1, keepdims=True) * (1.0 / MLA_Q_LORA) + EPS) * qg_ref[...]
    ckvn = ckv * lax.rsqrt(jnp.mean(ckv * ckv, axis=-1, keepdims=True) + EPS) * kvg_ref[...]
    cqb = (cqn * (MLA_SCALE * LOG2E)).astype(BF16)
    ckvb = ckvn.astype(BF16)
    cos, slo, shi = cos_ref[...], slo_ref[...], shi_ref[...]

    kpe_r = _rope(kpe, cos, slo, shi).astype(BF16)
    group = 4
    for hg in range(HEADS // group):
        qc = jnp.dot(cqb, wuq_ref[:, hg * group * MLA_QK_W:(hg + 1) * group * MLA_QK_W],
                     preferred_element_type=F32)
        kc = jnp.dot(ckvb, wukn_ref[:, hg * group * HEAD_W:(hg + 1) * group * HEAD_W],
                     preferred_element_type=F32)
        for hh in range(group):
            o = (hg * group + hh) * MLA_QK_W
            q_ref[:, o:o + HEAD_W] = qc[:, hh * MLA_QK_W:hh * MLA_QK_W + HEAD_W].astype(BF16)
            qp = qc[:, hh * MLA_QK_W + HEAD_W:(hh + 1) * MLA_QK_W]
            q_ref[:, o + HEAD_W:o + MLA_QK_W] = _rope(qp, cos, slo, shi).astype(BF16)
            k_ref[:, o:o + HEAD_W] = kc[:, hh * HEAD_W:(hh + 1) * HEAD_W].astype(BF16)
            k_ref[:, o + HEAD_W:o + MLA_QK_W] = kpe_r
    v_ref[...] = jnp.dot(ckvb, wuv_ref[...], preferred_element_type=F32).astype(BF16)


def _mla_proj(xa, g, mod, wd, qg, kvg, wuq, wukn, wuv, tabs, *, seq, n_batch):
    rows, d = xa.shape
    tm = MLA_ROW_TILE
    tiles_per_batch = seq // tm
    n_lat_tiles = n_batch * tiles_per_batch
    full = lambda arr: pl.BlockSpec(arr.shape, lambda i: (0, 0))
    tab_spec = pl.BlockSpec(
        (tm, HEAD_W), lambda i: (jnp.where(i < n_lat_tiles, i % tiles_per_batch, tiles_per_batch), 0))
    return pl.pallas_call(
        _mla_proj_kernel,
        grid=(rows // tm,),
        in_specs=[
            pl.BlockSpec((tm, d), lambda i: (i, 0)),
            full(g),
            pl.BlockSpec((1, 6, d), lambda i: (jnp.minimum(i // tiles_per_batch, n_batch), 0, 0)),
            full(wd), full(qg), full(kvg), full(wuq), full(wukn), full(wuv),
            tab_spec, tab_spec, tab_spec,
        ],
        out_specs=[
            pl.BlockSpec((tm, HEADS * MLA_QK_W), lambda i: (i, 0)),
            pl.BlockSpec((tm, HEADS * MLA_QK_W), lambda i: (i, 0)),
            pl.BlockSpec((tm, HEADS * HEAD_W), lambda i: (i, 0)),
        ],
        out_shape=[
            jax.ShapeDtypeStruct((rows, HEADS * MLA_QK_W), BF16),
            jax.ShapeDtypeStruct((rows, HEADS * MLA_QK_W), BF16),
            jax.ShapeDtypeStruct((rows, HEADS * HEAD_W), BF16),
        ],
        compiler_params=_cparams(("parallel",)),
        name="mla_proj",
    )(xa, g, mod, wd, qg, kvg, wuq, wukn, wuv, *tabs)


def _attn_kernel(*refs, diff, has_lat, key_tile, lam_init):
    refs = list(refs)
    q_ref, kc_ref, vc_ref = refs[:3]
    pos = 3
    if has_lat:
        kl_ref, vl_ref = refs[pos:pos + 2]
        pos += 2
    if diff:
        lam_ref, sg_ref = refs[pos:pos + 2]
        pos += 2
    o_ref, m_ref, acc_ref = refs[pos:pos + 3]
    pos += 3
    if has_lat:
        s_ref, p_ref, a_ref, c_ref = refs[pos:pos + 4]

    q = q_ref[...]
    tq = q.shape[0]
    if diff:
        lane = lax.broadcasted_iota(jnp.int32, q.shape, 1)
        zero = jnp.zeros_like(q)
        q = jnp.concatenate([jnp.where(lane < DIFF_HEAD_DIM, q, zero),
                             jnp.where(lane >= DIFF_HEAD_DIM, q, zero)], axis=0)

    def scores(k):
        return lax.dot_general(k, q, (((1,), (1,)), ((), ())), preferred_element_type=F32)

    def weighted(vt, p):
        lhs = jnp.concatenate([vt, jnp.ones((SUM_ROWS, vt.shape[1]), BF16)], axis=0)
        return jnp.dot(lhs, p, preferred_element_type=F32)

    s = scores(kc_ref[...])
    m0 = jnp.max(s, axis=0, keepdims=True)
    m_ref[...] = m0
    acc_ref[...] = weighted(vc_ref[...], jnp.exp2(s - m0).astype(BF16))

    if has_lat:
        n_chunks = kl_ref.shape[0] // key_tile

        def chunk(t):
            return pl.ds(pl.multiple_of(t * key_tile, key_tile), key_tile)

        def qk(t):
            s = scores(kl_ref[chunk(t), :])
            s_ref[...] = s
            c_ref[...] = jnp.max(s, axis=0, keepdims=True)

        def sm():
            m_prev = m_ref[...]
            m_new = jnp.maximum(m_prev, c_ref[...])
            a_ref[...] = jnp.exp2(m_prev - m_new)
            p_ref[...] = jnp.exp2(s_ref[...] - m_new).astype(BF16)
            m_ref[...] = m_new

        def pv(t):
            acc_ref[...] = a_ref[...] * acc_ref[...] + weighted(vl_ref[:, chunk(t)], p_ref[...])

        qk(0)
        sm()
        qk(1)

        def body(t, carry):
            pv(t - 1)
            sm()
            qk(t + 1)
            return carry

        lax.fori_loop(1, n_chunks - 1, body, 0, unroll=2)
        pv(n_chunks - 2)
        sm()
        pv(n_chunks - 1)

    acc = acc_ref[...]
    inv_l = 1.0 / acc[HEAD_W:HEAD_W + 1, :]
    acc = acc[:HEAD_W, :]
    if diff:
        lf = lam_ref[...]
        lam = (jnp.exp(jnp.sum(lf[0:1] * lf[1:2], axis=-1, keepdims=True))
               - jnp.exp(jnp.sum(lf[2:3] * lf[3:4], axis=-1, keepdims=True)) + lam_init)
        o = acc[:, :tq] * inv_l[:, :tq] - lam * (acc[:, tq:] * inv_l[:, tq:])
        ms = jnp.mean(o * o, axis=0, keepdims=True)
        o = o * lax.rsqrt(ms + EPS) * sg_ref[...] * (1.0 - lam_init)
    else:
        o = acc * inv_l
    o_ref[...] = o.T.astype(BF16)


def _attention(q_arr, k_arr, vt_arr, *, diff, seq, n_batch, n_ctx, q_col, k_col, qk_w,
               lat_queries, lam=None, subln_g=None, lam_init=0.0, prev_out=None):
    rows = q_arr.shape[0]
    ctx_row0 = n_batch * seq
    if lat_queries:
        tq = DIFF_Q_TILE if diff else MLA_Q_TILE
        n_q = seq // tq
        q_row = lambda b, qi: b * n_q + qi
    else:
        tq = n_ctx
        n_q = 1
        q_row = lambda b, qi: ctx_row0 // n_ctx + b
    nq_cols = 2 * tq if diff else tq

    in_specs = [
        pl.BlockSpec((tq, qk_w), lambda b, h, qi: (q_row(b, qi), q_col + h)),
        pl.BlockSpec((n_ctx, qk_w), lambda b, h, qi: (ctx_row0 // n_ctx + b, k_col + h)),
        pl.BlockSpec((HEAD_W, n_ctx), lambda b, h, qi: (h, ctx_row0 // n_ctx + b)),
    ]
    args = [q_arr, k_arr, vt_arr]
    scratch = [pltpu.VMEM((1, nq_cols), F32), pltpu.VMEM((HEAD_W + SUM_ROWS, nq_cols), F32)]
    if lat_queries:
        in_specs += [
            pl.BlockSpec((seq, qk_w), lambda b, h, qi: (b, k_col + h)),
            pl.BlockSpec((HEAD_W, seq), lambda b, h, qi: (h, b)),
        ]
        args += [k_arr, vt_arr]
        scratch += [pltpu.VMEM((KEY_TILE, nq_cols), F32), pltpu.VMEM((KEY_TILE, nq_cols), BF16),
                    pltpu.VMEM((1, nq_cols), F32), pltpu.VMEM((1, nq_cols), F32)]
    if diff:
        in_specs += [
            pl.BlockSpec(lam.shape, lambda b, h, qi: (0, 0)),
            pl.BlockSpec(subln_g.shape, lambda b, h, qi: (0, 0)),
        ]
        args += [lam, subln_g]
    aliases = {}
    if prev_out is not None:
        in_specs.append(pl.BlockSpec(memory_space=pl.ANY))
        args.append(prev_out)
        aliases = {len(args) - 1: 0}

    kern = functools.partial(_attn_kernel, diff=diff, has_lat=lat_queries, key_tile=KEY_TILE, lam_init=lam_init)
    if prev_out is not None:
        inner = kern
        kern = lambda *refs: inner(*refs[:len(args) - 1], *refs[len(args):])

    return pl.pallas_call(
        kern,
        grid=(n_batch, HEADS, n_q),
        in_specs=in_specs,
        out_specs=pl.BlockSpec((tq, HEAD_W), lambda b, h, qi: (q_row(b, qi), h)),
        out_shape=jax.ShapeDtypeStruct((rows, HEADS * HEAD_W), BF16),
        scratch_shapes=scratch,
        input_output_aliases=aliases,
        compiler_params=_cparams(("parallel", "parallel", "arbitrary")),
        name=("diff" if diff else "mla") + ("_attn_lat" if lat_queries else "_attn_ctx"),
    )(*args)


def _tail_kernel(o_ref, wo_ref, x_ref, g_ref, mod_ref, w1_ref, w2_ref, fg_ref, y_ref, h_ref, acc_ref, *, final):
    j = pl.program_id(1)

    @pl.when(j == 0)
    def _():
        x1 = x_ref[...] + mod_ref[0, 2:3, :] * jnp.dot(o_ref[...], wo_ref[...], preferred_element_type=F32)
        y_ref[...] = x1
        h = _norm_mod(x1, g_ref[...], mod_ref[0, 3:4, :], mod_ref[0, 4:5, :])
        h_ref[...] = h.astype(BF16)
        acc_ref[...] = jnp.zeros_like(acc_ref)

    a = jnp.dot(h_ref[...], w1_ref[...], preferred_element_type=F32)
    a = jnp.maximum(a, 0.0)
    acc_ref[...] += jnp.dot((a * a).astype(BF16), w2_ref[...], preferred_element_type=F32)

    @pl.when(j == pl.num_programs(1) - 1)
    def _():
        y = y_ref[...] + mod_ref[0, 5:6, :] * acc_ref[...]
        if final:
            ms = jnp.mean(y * y, axis=-1, keepdims=True)
            y = y * lax.rsqrt(ms + EPS) * fg_ref[...]
        y_ref[...] = y


def _tail(o, wo, xa, g, mod, w1, w2, final_g, *, n_tiles, tiles_per_batch, n_batch, final):
    d = xa.shape[1]
    ff = w1.shape[1]
    tm, tf = ROW_TILE, FF_TILE
    once = pl.Buffered(1)
    return pl.pallas_call(
        functools.partial(_tail_kernel, final=final),
        grid=(n_tiles, ff // tf),
        in_specs=[
            pl.BlockSpec((tm, o.shape[1]), lambda i, j: (i, 0)),
            pl.BlockSpec(wo.shape, lambda i, j: (0, 0), pipeline_mode=once),
            pl.BlockSpec((tm, d), lambda i, j: (i, 0)),
            pl.BlockSpec((1, d), lambda i, j: (0, 0)),
            pl.BlockSpec((1, 6, d), lambda i, j: (jnp.minimum(i // tiles_per_batch, n_batch), 0, 0)),
            pl.BlockSpec((d, tf), lambda i, j: (0, j)),
            pl.BlockSpec((tf, d), lambda i, j: (j, 0)),
            pl.BlockSpec((1, d), lambda i, j: (0, 0)),
        ],
        out_specs=pl.BlockSpec((tm, d), lambda i, j: (i, 0)),
        out_shape=jax.ShapeDtypeStruct((n_tiles * tm, d), F32),
        scratch_shapes=[pltpu.VMEM((tm, d), BF16), pltpu.VMEM((tm, d), F32)],
        compiler_params=_cparams(("parallel", "arbitrary")),
        name="tail",
    )(o, wo, xa, g, mod, w1, w2, final_g)


def _mla_weights(wdown, q_norm_g, wuq, kv_norm_g, wukv):
    d = wdown.shape[0]
    qpad = MLA_Q_PAD - MLA_Q_LORA
    z = lambda n: jnp.zeros((d, n), wdown.dtype)
    kv_end = MLA_Q_LORA + MLA_KV_LORA
    wd = jnp.concatenate([wdown[:, :MLA_Q_LORA], z(qpad), wdown[:, MLA_Q_LORA:kv_end],
                          wdown[:, kv_end:], z(HEAD_W - MLA_ROPE)], axis=1).astype(BF16)
    qg = jnp.pad(q_norm_g, (0, qpad)).reshape(1, -1)
    kvg = kv_norm_g.reshape(1, -1)
    wq = wuq.reshape(MLA_Q_LORA, HEADS, MLA_NOPE + MLA_ROPE)
    wq = jnp.pad(wq, ((0, qpad), (0, 0), (0, MLA_QK_W - MLA_NOPE - MLA_ROPE)))
    wq = wq.reshape(MLA_Q_PAD, HEADS * MLA_QK_W).astype(BF16)
    wkv = wukv.reshape(MLA_KV_LORA, HEADS, 2 * HEAD_W)
    wukn = wkv[:, :, :HEAD_W].reshape(MLA_KV_LORA, HEADS * HEAD_W).astype(BF16)
    wuv = wkv[:, :, HEAD_W:].reshape(MLA_KV_LORA, HEADS * HEAD_W).astype(BF16)
    return wd, qg, kvg, wq, wukn, wuv


def kernel(x, c, ctx, c_ctx, ada_w, ada_b, norm_mix_g, norm_mlp_g, dif_wqkv, dif_wo, dif_lambda, dif_subln_g,
           mla_wdown, mla_q_norm_g, mla_wuq, mla_kv_norm_g, mla_wukv, mla_wo, mlp_w1, mlp_w2, final_g):
    n_batch, seq, d = x.shape
    n_ctx = ctx.shape[1]
    depth = ada_w.shape[0]
    assert d == HEADS * HEAD_W and seq % ROW_TILE == 0 and (n_batch * n_ctx) % ROW_TILE == 0
    assert seq % (2 * KEY_TILE) == 0 and n_batch + 1 <= 8 and MLA_ROW_TILE == n_ctx

    tiles_per_batch = seq // ROW_TILE
    n_lat_tiles = n_batch * tiles_per_batch
    n_all_tiles = n_lat_tiles + n_batch * n_ctx // ROW_TILE
    geom = dict(tiles_per_batch=tiles_per_batch, n_batch=n_batch)

    cond = jnp.concatenate([c, c_ctx[None, :], jnp.zeros((8 - n_batch - 1, d), F32)], axis=0)
    mods = _ada_all(cond, ada_w, ada_b)[:, :n_batch + 1].reshape(depth, n_batch + 1, 6, d)

    tabs = _rope_tables(seq, ROW_TILE)
    xa = jnp.concatenate([x.reshape(n_batch * seq, d), ctx.reshape(n_batch * n_ctx, d)], axis=0)
    row = lambda v: v.reshape(1, -1)

    for i in range(depth):
        last = i == depth - 1
        j = i // N_MIXERS
        mod = mods[i]
        att = dict(seq=seq, n_batch=n_batch, n_ctx=n_ctx)
        if i % N_MIXERS == 0:
            qkv = _qkv_proj(xa, row(norm_mix_g[i]), mod, dif_wqkv[j].astype(BF16), tabs,
                            n_lat_tiles=n_lat_tiles, **geom)
            att.update(diff=True, q_col=0, k_col=HEADS, qk_w=HEAD_W, lam=dif_lambda[j],
                       subln_g=dif_subln_g[j].reshape(-1, 1), lam_init=_diff_lambda_init(i))
            vt = qkv[:, 2 * d:].T
            o = _attention(qkv, qkv, vt, lat_queries=True, **att)
            if not last:
                o = _attention(qkv, qkv, vt, lat_queries=False, prev_out=o, **att)
            wo = dif_wo[j]
        else:
            wts = _mla_weights(mla_wdown[j], mla_q_norm_g[j], mla_wuq[j], mla_kv_norm_g[j], mla_wukv[j])
            qa, ka, va = _mla_proj(xa, row(norm_mix_g[i]), mod, *wts, tabs, seq=seq, n_batch=n_batch)
            att.update(diff=False, q_col=0, k_col=0, qk_w=MLA_QK_W)
            vt = va.T
            o = _attention(qa, ka, vt, lat_queries=True, **att)
            if not last:
                o = _attention(qa, ka, vt, lat_queries=False, prev_out=o, **att)
            wo = mla_wo[j]
        n_tiles = n_lat_tiles if last else n_all_tiles
        xa = _tail(o, wo.astype(BF16), xa, row(norm_mlp_g[i]), mod, mlp_w1[i].astype(BF16), mlp_w2[i].astype(BF16),
                   row(final_g), n_tiles=n_tiles, final=last, **geom)

    return xa.reshape(n_batch, seq, d)
```

```python
import functools
import math

import jax
import jax.numpy as jnp
from jax import lax
from jax.experimental import pallas as pl
from jax.experimental.pallas import tpu as pltpu

F32 = jnp.float32
BF16 = jnp.bfloat16

GRID_W = 64
N_MIXERS = 2
HEADS = 16
HEAD_W = 128
DIFF_HEAD_DIM = 64
MLA_NOPE = 128
MLA_ROPE = 64
MLA_Q_LORA = 448
MLA_KV_LORA = 512
MLA_Q_PAD = 512
MLA_QK_W = 256
DIFF_SCALE = DIFF_HEAD_DIM ** -0.5
MLA_SCALE = (MLA_NOPE + MLA_ROPE) ** -0.5
ROPE_THETA = 10000.0
ROPE_QUARTER = 16
EPS = 1e-6
LOG2E = math.log2(math.e)
SUM_ROWS = 16

ROW_TILE = 512
MLA_ROW_TILE = 256
ADA_COL_TILE = 1024
QKV_COL_TILE = 1024
FF_TILE = 1024
DIFF_Q_TILE = 1024
MLA_Q_TILE = 2048
KEY_TILE = 1024
VMEM_LIMIT = 58 * 1024 * 1024


def _cparams(sem):
    return pltpu.CompilerParams(dimension_semantics=sem, vmem_limit_bytes=VMEM_LIMIT)


def _diff_lambda_init(layer):
    return 0.8 - 0.6 * math.exp(-0.3 * layer)


def _ada_kernel(s_ref, w_ref, b_ref, o_ref):
    s = s_ref[...]
    s = s * (1.0 / (1.0 + jnp.exp(-s)))
    o_ref[0] = jnp.dot(s, w_ref[0], preferred_element_type=F32,
                       precision=lax.Precision.HIGHEST) + b_ref[0]


def _ada_all(cond, ada_w, ada_b):
    depth, d, n = ada_w.shape
    return pl.pallas_call(
        _ada_kernel,
        grid=(depth, n // ADA_COL_TILE),
        in_specs=[
            pl.BlockSpec((8, d), lambda i, j: (0, 0)),
            pl.BlockSpec((1, d, ADA_COL_TILE), lambda i, j: (i, 0, j)),
            pl.BlockSpec((1, 1, ADA_COL_TILE), lambda i, j: (i, 0, j)),
        ],
        out_specs=pl.BlockSpec((1, 8, ADA_COL_TILE), lambda i, j: (i, 0, j)),
        out_shape=jax.ShapeDtypeStruct((depth, 8, n), F32),
        compiler_params=_cparams(("parallel", "parallel")),
        name="ada",
    )(cond, ada_w, ada_b.reshape(depth, 1, n))


def _norm_mod(x, g, shift, scale):
    ms = jnp.mean(x * x, axis=-1, keepdims=True)
    y = x * lax.rsqrt(ms + EPS) * g
    return y * (1.0 + scale) + shift


def _rope(a, cos, sin_lo, sin_hi):
    up = pltpu.roll(a, HEAD_W - ROPE_QUARTER, 1)
    dn = pltpu.roll(a, ROPE_QUARTER, 1)
    return a * cos + up * sin_lo + dn * sin_hi


def _rope_tables(seq, pad_rows):
    t = jnp.arange(seq)
    row = (t // GRID_W).astype(F32)[:, None]
    col = (t % GRID_W).astype(F32)[:, None]
    inv = ROPE_THETA ** (-jnp.arange(ROPE_QUARTER, dtype=F32) / ROPE_QUARTER)
    ang = jnp.concatenate([row * inv, row * inv, col * inv, col * inv], axis=1)
    cos, sin = jnp.cos(ang), jnp.sin(ang)
    first = (jnp.arange(4 * ROPE_QUARTER) % (2 * ROPE_QUARTER)) < ROPE_QUARTER
    sin_lo = jnp.where(first, -sin, 0.0)
    sin_hi = jnp.where(first, 0.0, sin)

    def finish(tab, fill):
        tab = jnp.concatenate([tab, tab], axis=1)
        return jnp.concatenate([tab, jnp.full((pad_rows, HEAD_W), fill, F32)], axis=0)

    return finish(cos, 1.0), finish(sin_lo, 0.0), finish(sin_hi, 0.0)


def _qkv_kernel(x_ref, g_ref, mod_ref, w_ref, cos_ref, slo_ref, shi_ref, o_ref, h_ref, acc_ref, *,
                n_col, q_tiles, n_steps):
    t = pl.program_id(0)
    j = jnp.minimum(t, n_steps - 2) % n_col
    jp = jnp.maximum(t - 1, 0) % n_col

    @pl.when(t == 0)
    def _():
        acc_ref[...] = jnp.zeros_like(acc_ref)

    @pl.when(j == 0)
    def _():
        h = _norm_mod(x_ref[...], g_ref[...], mod_ref[0, 0:1, :], mod_ref[0, 1:2, :])
        h_ref[...] = h.astype(BF16)

    sc = jnp.where(jp < q_tiles, DIFF_SCALE * LOG2E, 1.0).astype(F32)
    cos, slo, shi = cos_ref[...], slo_ref[...], shi_ref[...]
    for c in range(acc_ref.shape[1] // HEAD_W):
        a = acc_ref[:, c * HEAD_W:(c + 1) * HEAD_W]
        o_ref[:, c * HEAD_W:(c + 1) * HEAD_W] = (_rope(a, cos, slo, shi) * sc).astype(BF16)
    acc_ref[...] = jnp.dot(h_ref[...], w_ref[...], preferred_element_type=F32)


def _qkv_proj(xa, g, mod, w, tabs, *, tiles_per_batch, n_lat_tiles, n_batch):
    rows, d = xa.shape
    n = w.shape[1]
    tm, tn = ROW_TILE, QKV_COL_TILE
    n_col = n // tn
    n_steps = (rows // tm) * n_col + 1
    rope_tiles = 2 * d // tn
    cur = lambda t: jnp.minimum(t, n_steps - 2)
    prev = lambda t: jnp.maximum(t - 1, 0)

    def tab_map(t):
        ip, jp = prev(t) // n_col, prev(t) % n_col
        return (jnp.where((ip < n_lat_tiles) & (jp < rope_tiles), ip % tiles_per_batch, tiles_per_batch), 0)

    tab_spec = pl.BlockSpec((tm, HEAD_W), tab_map)
    kern = functools.partial(_qkv_kernel, n_col=n_col, q_tiles=d // tn, n_steps=n_steps)
    return pl.pallas_call(
        kern,
        grid=(n_steps,),
        in_specs=[
            pl.BlockSpec((tm, d), lambda t: (cur(t) // n_col, 0)),
            pl.BlockSpec((1, d), lambda t: (0, 0)),
            pl.BlockSpec((1, 6, d), lambda t: (jnp.minimum(cur(t) // n_col // tiles_per_batch, n_batch), 0, 0)),
            pl.BlockSpec((d, tn), lambda t: (0, cur(t) % n_col)),
            tab_spec, tab_spec, tab_spec,
        ],
        out_specs=pl.BlockSpec((tm, tn), lambda t: (prev(t) // n_col, prev(t) % n_col)),
        out_shape=jax.ShapeDtypeStruct((rows, n), BF16),
        scratch_shapes=[pltpu.VMEM((tm, d), BF16), pltpu.VMEM((tm, tn), F32)],
        compiler_params=_cparams(("arbitrary",)),
        name="diff_qkv",
    )(xa, g, mod, w, *tabs)


def _mla_proj_kernel(x_ref, g_ref, mod_ref, wd_ref, qg_ref, kvg_ref, wuq_ref, wukn_ref, wuv_ref,
                     cos_ref, slo_ref, shi_ref, q_ref, k_ref, v_ref):
    h = _norm_mod(x_ref[...], g_ref[...], mod_ref[0, 0:1, :], mod_ref[0, 1:2, :]).astype(BF16)
    a = jnp.dot(h, wd_ref[...], preferred_element_type=F32)
    cq = a[:, :MLA_Q_PAD]
    ckv = a[:, MLA_Q_PAD:MLA_Q_PAD + MLA_KV_LORA]
    kpe = a[:, MLA_Q_PAD + MLA_KV_LORA:]
    cqn = cq * lax.rsqrt(jnp.sum(cq * cq, axis=-1, keepdims=True) * (1.0 / MLA_Q_LORA) + EPS) * qg_ref[...]
    ckvn = ckv * lax.rsqrt(jnp.mean(ckv * ckv, axis=-1, keepdims=True) + EPS) * kvg_ref[...]
    cqb = (cqn * (MLA_SCALE * LOG2E)).astype(BF16)
    ckvb = ckvn.astype(BF16)
    cos, slo, shi = cos_ref[...], slo_ref[...], shi_ref[...]

    kpe_r = _rope(kpe, cos, slo, shi).astype(BF16)
    group = 4
    for hg in range(HEADS // group):
        qc = jnp.dot(cqb, wuq_ref[:, hg * group * MLA_QK_W:(hg + 1) * group * MLA_QK_W],
                     preferred_element_type=F32)
        kc = jnp.dot(ckvb, wukn_ref[:, hg * group * HEAD_W:(hg + 1) * group * HEAD_W],
                     preferred_element_type=F32)
        for hh in range(group):
            o = (hg * group + hh) * MLA_QK_W
            q_ref[:, o:o + HEAD_W] = qc[:, hh * MLA_QK_W:hh * MLA_QK_W + HEAD_W].astype(BF16)
            qp = qc[:, hh * MLA_QK_W + HEAD_W:(hh + 1) * MLA_QK_W]
            q_ref[:, o + HEAD_W:o + MLA_QK_W] = _rope(qp, cos, slo, shi).astype(BF16)
            k_ref[:, o:o + HEAD_W] = kc[:, hh * HEAD_W:(hh + 1) * HEAD_W].astype(BF16)
            k_ref[:, o + HEAD_W:o + MLA_QK_W] = kpe_r
    v_ref[...] = jnp.dot(ckvb, wuv_ref[...], preferred_element_type=F32).astype(BF16)


def _mla_proj(xa, g, mod, wd, qg, kvg, wuq, wukn, wuv, tabs, *, seq, n_batch):
    rows, d = xa.shape
    tm = MLA_ROW_TILE
    tiles_per_batch = seq // tm
    n_lat_tiles = n_batch * tiles_per_batch
    full = lambda arr: pl.BlockSpec(arr.shape, lambda i: (0, 0))
    tab_spec = pl.BlockSpec(
        (tm, HEAD_W), lambda i: (jnp.where(i < n_lat_tiles, i % tiles_per_batch, tiles_per_batch), 0))
    return pl.pallas_call(
        _mla_proj_kernel,
        grid=(rows // tm,),
        in_specs=[
            pl.BlockSpec((tm, d), lambda i: (i, 0)),
            full(g),
            pl.BlockSpec((1, 6, d), lambda i: (jnp.minimum(i // tiles_per_batch, n_batch), 0, 0)),
            full(wd), full(qg), full(kvg), full(wuq), full(wukn), full(wuv),
            tab_spec, tab_spec, tab_spec,
        ],
        out_specs=[
            pl.BlockSpec((tm, HEADS * MLA_QK_W), lambda i: (i, 0)),
            pl.BlockSpec((tm, HEADS * MLA_QK_W), lambda i: (i, 0)),
            pl.BlockSpec((tm, HEADS * HEAD_W), lambda i: (i, 0)),
        ],
        out_shape=[
            jax.ShapeDtypeStruct((rows, HEADS * MLA_QK_W), BF16),
            jax.ShapeDtypeStruct((rows, HEADS * MLA_QK_W), BF16),
            jax.ShapeDtypeStruct((rows, HEADS * HEAD_W), BF16),
        ],
        compiler_params=_cparams(("parallel",)),
        name="mla_proj",
    )(xa, g, mod, wd, qg, kvg, wuq, wukn, wuv, *tabs)


def _attn_kernel(*refs, diff, has_lat, key_tile, lam_init):
    refs = list(refs)
    q_ref, kc_ref, vc_ref = refs[:3]
    pos = 3
    if has_lat:
        kl_ref, vl_ref = refs[pos:pos + 2]
        pos += 2
    if diff:
        lam_ref, sg_ref = refs[pos:pos + 2]
        pos += 2
    o_ref, m_ref, acc_ref = refs[pos:pos + 3]
    pos += 3
    if has_lat:
        s_ref, p_ref, a_ref, c_ref = refs[pos:pos + 4]

    q = q_ref[...]
    tq = q.shape[0]
    if diff:
        lane = lax.broadcasted_iota(jnp.int32, q.shape, 1)
        zero = jnp.zeros_like(q)
        q = jnp.concatenate([jnp.where(lane < DIFF_HEAD_DIM, q, zero),
                             jnp.where(lane >= DIFF_HEAD_DIM, q, zero)], axis=0)

    def scores(k):
        return lax.dot_general(k, q, (((1,), (1,)), ((), ())), preferred_element_type=F32)

    def weighted(vt, p):
        lhs = jnp.concatenate([vt, jnp.ones((SUM_ROWS, vt.shape[1]), BF16)], axis=0)
        return jnp.dot(lhs, p, preferred_element_type=F32)

    s = scores(kc_ref[...])
    m0 = jnp.max(s, axis=0, keepdims=True)
    m_ref[...] = m0
    acc_ref[...] = weighted(vc_ref[...], jnp.exp2(s - m0).astype(BF16))

    if has_lat:
        n_chunks = kl_ref.shape[0] // key_tile

        def chunk(t):
            return pl.ds(pl.multiple_of(t * key_tile, key_tile), key_tile)

        def qk(t):
            s = scores(kl_ref[chunk(t), :])
            s_ref[...] = s
            c_ref[...] = jnp.max(s, axis=0, keepdims=True)

        def sm():
            m_prev = m_ref[...]
            m_new = jnp.maximum(m_prev, c_ref[...])
            a_ref[...] = jnp.exp2(m_prev - m_new)
            p_ref[...] = jnp.exp2(s_ref[...] - m_new).astype(BF16)
            m_ref[...] = m_new

        def pv(t):
            acc_ref[...] = a_ref[...] * acc_ref[...] + weighted(vl_ref[:, chunk(t)], p_ref[...])

        qk(0)
        sm()
        qk(1)

        def body(t, carry):
            pv(t - 1)
            sm()
            qk(t + 1)
            return carry

        lax.fori_loop(1, n_chunks - 1, body, 0, unroll=2)
        pv(n_chunks - 2)
        sm()
        pv(n_chunks - 1)

    acc = acc_ref[...]
    inv_l = 1.0 / acc[HEAD_W:HEAD_W + 1, :]
    acc = acc[:HEAD_W, :]
    if diff:
        lf = lam_ref[...]
        lam = (jnp.exp(jnp.sum(lf[0:1] * lf[1:2], axis=-1, keepdims=True))
               - jnp.exp(jnp.sum(lf[2:3] * lf[3:4], axis=-1, keepdims=True)) + lam_init)
        o = acc[:, :tq] * inv_l[:, :tq] - lam * (acc[:, tq:] * inv_l[:, tq:])
        ms = jnp.mean(o * o, axis=0, keepdims=True)
        o = o * lax.rsqrt(ms + EPS) * sg_ref[...] * (1.0 - lam_init)
    else:
        o = acc * inv_l
    o_ref[...] = o.T.astype(BF16)


def _attention(q_arr, k_arr, vt_arr, *, diff, seq, n_batch, n_ctx, q_col, k_col, qk_w,
               lat_queries, lam=None, subln_g=None, lam_init=0.0, prev_out=None):
    rows = q_arr.shape[0]
    ctx_row0 = n_batch * seq
    if lat_queries:
        tq = DIFF_Q_TILE if diff else MLA_Q_TILE
        n_q = seq // tq
        q_row = lambda b, qi: b * n_q + qi
    else:
        tq = n_ctx
        n_q = 1
        q_row = lambda b, qi: ctx_row0 // n_ctx + b
    nq_cols = 2 * tq if diff else tq

    in_specs = [
        pl.BlockSpec((tq, qk_w), lambda b, h, qi: (q_row(b, qi), q_col + h)),
        pl.BlockSpec((n_ctx, qk_w), lambda b, h, qi: (ctx_row0 // n_ctx + b, k_col + h)),
        pl.BlockSpec((HEAD_W, n_ctx), lambda b, h, qi: (h, ctx_row0 // n_ctx + b)),
    ]
    args = [q_arr, k_arr, vt_arr]
    scratch = [pltpu.VMEM((1, nq_cols), F32), pltpu.VMEM((HEAD_W + SUM_ROWS, nq_cols), F32)]
    if lat_queries:
        in_specs += [
            pl.BlockSpec((seq, qk_w), lambda b, h, qi: (b, k_col + h)),
            pl.BlockSpec((HEAD_W, seq), lambda b, h, qi: (h, b)),
        ]
        args += [k_arr, vt_arr]
        scratch += [pltpu.VMEM((KEY_TILE, nq_cols), F32), pltpu.VMEM((KEY_TILE, nq_cols), BF16),
                    pltpu.VMEM((1, nq_cols), F32), pltpu.VMEM((1, nq_cols), F32)]
    if diff:
        in_specs += [
            pl.BlockSpec(lam.shape, lambda b, h, qi: (0, 0)),
            pl.BlockSpec(subln_g.shape, lambda b, h, qi: (0, 0)),
        ]
        args += [lam, subln_g]
    aliases = {}
    if prev_out is not None:
        in_specs.append(pl.BlockSpec(memory_space=pl.ANY))
        args.append(prev_out)
        aliases = {len(args) - 1: 0}

    kern = functools.partial(_attn_kernel, diff=diff, has_lat=lat_queries, key_tile=KEY_TILE, lam_init=lam_init)
    if prev_out is not None:
        inner = kern
        kern = lambda *refs: inner(*refs[:len(args) - 1], *refs[len(args):])

    return pl.pallas_call(
        kern,
        grid=(n_batch, HEADS, n_q),
        in_specs=in_specs,
        out_specs=pl.BlockSpec((tq, HEAD_W), lambda b, h, qi: (q_row(b, qi), h)),
        out_shape=jax.ShapeDtypeStruct((rows, HEADS * HEAD_W), BF16),
        scratch_shapes=scratch,
        input_output_aliases=aliases,
        compiler_params=_cparams(("parallel", "parallel", "arbitrary")),
        name=("diff" if diff else "mla") + ("_attn_lat" if lat_queries else "_attn_ctx"),
    )(*args)


def _tail_kernel(o_ref, wo_ref, x_ref, g_ref, mod_ref, w1_ref, w2_ref, fg_ref, y_ref, h_ref, acc_ref, *, final):
    j = pl.program_id(1)

    @pl.when(j == 0)
    def _():
        x1 = x_ref[...] + mod_ref[0, 2:3, :] * jnp.dot(o_ref[...], wo_ref[...], preferred_element_type=F32)
        y_ref[...] = x1
        h = _norm_mod(x1, g_ref[...], mod_ref[0, 3:4, :], mod_ref[0, 4:5, :])
        h_ref[...] = h.astype(BF16)
        acc_ref[...] = jnp.zeros_like(acc_ref)

    a = jnp.dot(h_ref[...], w1_ref[...], preferred_element_type=F32)
    a = jnp.maximum(a, 0.0)
    acc_ref[...] += jnp.dot((a * a).astype(BF16), w2_ref[...], preferred_element_type=F32)

    @pl.when(j == pl.num_programs(1) - 1)
    def _():
        y = y_ref[...] + mod_ref[0, 5:6, :] * acc_ref[...]
        if final:
            ms = jnp.mean(y * y, axis=-1, keepdims=True)
            y = y * lax.rsqrt(ms + EPS) * fg_ref[...]
        y_ref[...] = y


def _tail(o, wo, xa, g, mod, w1, w2, final_g, *, n_tiles, tiles_per_batch, n_batch, final):
    d = xa.shape[1]
    ff = w1.shape[1]
    tm, tf = ROW_TILE, FF_TILE
    once = pl.Buffered(1)
    return pl.pallas_call(
        functools.partial(_tail_kernel, final=final),
        grid=(n_tiles, ff // tf),
        in_specs=[
            pl.BlockSpec((tm, o.shape[1]), lambda i, j: (i, 0)),
            pl.BlockSpec(wo.shape, lambda i, j: (0, 0), pipeline_mode=once),
            pl.BlockSpec((tm, d), lambda i, j: (i, 0)),
            pl.BlockSpec((1, d), lambda i, j: (0, 0)),
            pl.BlockSpec((1, 6, d), lambda i, j: (jnp.minimum(i // tiles_per_batch, n_batch), 0, 0)),
            pl.BlockSpec((d, tf), lambda i, j: (0, j)),
            pl.BlockSpec((tf, d), lambda i, j: (j, 0)),
            pl.BlockSpec((1, d), lambda i, j: (0, 0)),
        ],
        out_specs=pl.BlockSpec((tm, d), lambda i, j: (i, 0)),
        out_shape=jax.ShapeDtypeStruct((n_tiles * tm, d), F32),
        scratch_shapes=[pltpu.VMEM((tm, d), BF16), pltpu.VMEM((tm, d), F32)],
        compiler_params=_cparams(("parallel", "arbitrary")),
        name="tail",
    )(o, wo, xa, g, mod, w1, w2, final_g)


def _mla_weights(wdown, q_norm_g, wuq, kv_norm_g, wukv):
    d = wdown.shape[0]
    qpad = MLA_Q_PAD - MLA_Q_LORA
    z = lambda n: jnp.zeros((d, n), wdown.dtype)
    kv_end = MLA_Q_LORA + MLA_KV_LORA
    wd = jnp.concatenate([wdown[:, :MLA_Q_LORA], z(qpad), wdown[:, MLA_Q_LORA:kv_end],
                          wdown[:, kv_end:], z(HEAD_W - MLA_ROPE)], axis=1).astype(BF16)
    qg = jnp.pad(q_norm_g, (0, qpad)).reshape(1, -1)
    kvg = kv_norm_g.reshape(1, -1)
    wq = wuq.reshape(MLA_Q_LORA, HEADS, MLA_NOPE + MLA_ROPE)
    wq = jnp.pad(wq, ((0, qpad), (0, 0), (0, MLA_QK_W - MLA_NOPE - MLA_ROPE)))
    wq = wq.reshape(MLA_Q_PAD, HEADS * MLA_QK_W).astype(BF16)
    wkv = wukv.reshape(MLA_KV_LORA, HEADS, 2 * HEAD_W)
    wukn = wkv[:, :, :HEAD_W].reshape(MLA_KV_LORA, HEADS * HEAD_W).astype(BF16)
    wuv = wkv[:, :, HEAD_W:].reshape(MLA_KV_LORA, HEADS * HEAD_W).astype(BF16)
    return wd, qg, kvg, wq, wukn, wuv


def kernel(x, c, ctx, c_ctx, ada_w, ada_b, norm_mix_g, norm_mlp_g, dif_wqkv, dif_wo, dif_lambda, dif_subln_g,
           mla_wdown, mla_q_norm_g, mla_wuq, mla_kv_norm_g, mla_wukv, mla_wo, mlp_w1, mlp_w2, final_g):
    n_batch, seq, d = x.shape
    n_ctx = ctx.shape[1]
    depth = ada_w.shape[0]
    assert d == HEADS * HEAD_W and seq % ROW_TILE == 0 and (n_batch * n_ctx) % ROW_TILE == 0
    assert seq % (2 * KEY_TILE) == 0 and n_batch + 1 <= 8 and MLA_ROW_TILE == n_ctx

    tiles_per_batch = seq // ROW_TILE
    n_lat_tiles = n_batch * tiles_per_batch
    n_all_tiles = n_lat_tiles + n_batch * n_ctx // ROW_TILE
    geom = dict(tiles_per_batch=tiles_per_batch, n_batch=n_batch)

    cond = jnp.concatenate([c, c_ctx[None, :], jnp.zeros((8 - n_batch - 1, d), F32)], axis=0)
    mods = _ada_all(cond, ada_w, ada_b)[:, :n_batch + 1].reshape(depth, n_batch + 1, 6, d)

    tabs = _rope_tables(seq, ROW_TILE)
    xa = jnp.concatenate([x.reshape(n_batch * seq, d), ctx.reshape(n_batch * n_ctx, d)], axis=0)
    row = lambda v: v.reshape(1, -1)

    for i in range(depth):
        last = i == depth - 1
        j = i // N_MIXERS
        mod = mods[i]
        att = dict(seq=seq, n_batch=n_batch, n_ctx=n_ctx)
        if i % N_MIXERS == 0:
            qkv = _qkv_proj(xa, row(norm_mix_g[i]), mod, dif_wqkv[j].astype(BF16), tabs,
                            n_lat_tiles=n_lat_tiles, **geom)
            att.update(diff=True, q_col=0, k_col=HEADS, qk_w=HEAD_W, lam=dif_lambda[j],
                       subln_g=dif_subln_g[j].reshape(-1, 1), lam_init=_diff_lambda_init(i))
            vt = qkv[:, 2 * d:].T
            o = _attention(qkv, qkv, vt, lat_queries=True, **att)
            if not last:
                o = _attention(qkv, qkv, vt, lat_queries=False, prev_out=o, **att)
            wo = dif_wo[j]
        else:
            wts = _mla_weights(mla_wdown[j], mla_q_norm_g[j], mla_wuq[j], mla_kv_norm_g[j], mla_wukv[j])
            qa, ka, va = _mla_proj(xa, row(norm_mix_g[i]), mod, *wts, tabs, seq=seq, n_batch=n_batch)
            att.update(diff=False, q_col=0, k_col=0, qk_w=MLA_QK_W)
            vt = va.T
            o = _attention(qa, ka, vt, lat_queries=True, **att)
            if not last:
                o = _attention(qa, ka, vt, lat_queries=False, prev_out=o, **att)
            wo = mla_wo[j]
        n_tiles = n_lat_tiles if last else n_all_tiles
        xa = _tail(o, wo.astype(BF16), xa, row(norm_mlp_g[i]), mod, mlp_w1[i].astype(BF16), mlp_w2[i].astype(BF16),
                   row(final_g), n_tiles=n_tiles, final=last, **geom)

    return xa.reshape(n_batch, seq, d)
```
